```python
import math
import jax, jax.numpy as jnp
from jax import lax
import numpy as np

D_MODEL = 1024
BATCH = 4
SEQ = 4096
DEPTH = 1
DEC_BATCH = 8
DEC_SEQ = 64
PAST_LEN = 1024

CHUNK = 64
EPS = 1e-6
A_HEADS = 8
A_DK = 128
A_DV = 128
A_QK = A_HEADS * A_DK
A_V = A_HEADS * A_DV
A_CONV_DIM = 2 * A_QK + A_V
CONV_W = 4
B_HEADS = 8
B_DH = 128
B_W = B_HEADS * B_DH
Q_BLOCK = 128
N_GROUPS = 4
EXPERTS_PER_GROUP = 8
TOP_K_IN_GROUP = 2
D_EXPERT = 512
IN_COLS = A_CONV_DIM + A_V + 2 * A_HEADS + 3 * B_W + 2 * D_MODEL

kernel_name = 'hybrid_deltanet_stickbreaking_hmoe_stream_step'


def _in_splits():
    widths = (A_CONV_DIM, A_V, A_HEADS, A_HEADS, B_W, B_W, B_W, D_MODEL)
    pts, acc = [], 0
    for w in widths:
        acc += w
        pts.append(acc)
    return pts


def _rms(x, g):
    xf = x.astype(jnp.float32)
    y = xf * lax.rsqrt(jnp.mean(xf * xf, axis=-1, keepdims=True) + EPS) * g.astype(jnp.float32)
    return y.astype(x.dtype)


def _l2(x):
    xf = x.astype(jnp.float32)
    return xf * lax.rsqrt(jnp.sum(xf * xf, axis=-1, keepdims=True) + EPS)


def _causal_conv(u, hist, w):
    L = u.shape[1]
    up = jnp.concatenate([hist.astype(u.dtype), u], axis=1)
    acc = w[0] * up[:, 0:L]
    for i in range(1, CONV_W):
        acc = acc + w[i] * up[:, i:i + L]
    return jax.nn.silu(acc), up[:, L:]


def _gated_delta(q, k, v, g, beta, S0):
    Bn, L, H, DK = q.shape
    DV = v.shape[-1]
    C = min(CHUNK, L)
    N = L // C

    def blocks(t):
        t = t.reshape((Bn, N, C, H) + t.shape[3:])
        return jnp.moveaxis(t, (1, 3), (0, 2))

    qc = blocks(q * (DK ** -0.5))
    kc = blocks(k)
    vc = blocks(v)
    gc = blocks(g)
    bc = blocks(beta)
    G = jnp.cumsum(gc, axis=-1)
    i = jnp.arange(C)
    incl = i[:, None] >= i[None, :]
    strict = i[:, None] > i[None, :]
    decay = jnp.exp(jnp.where(incl, G[..., :, None] - G[..., None, :], -jnp.inf))
    kk = jnp.einsum('nbhck,nbhsk->nbhcs', kc, kc)
    Lm = jnp.where(strict, bc[..., :, None] * kk * decay, 0.0)
    Im = jnp.eye(C, dtype=jnp.float32) + Lm

    def solve(rhs):
        return lax.linalg.triangular_solve(Im, rhs, left_side=True, lower=True, unit_diagonal=True)

    u_bar = solve(bc[..., None] * vc)
    w_k = solve((bc * jnp.exp(G))[..., None] * kc)
    a_qk = jnp.einsum('nbhck,nbhsk->nbhcs', qc, kc) * decay
    g_last = G[..., -1]
    k_tail = kc * jnp.exp(g_last[..., None] - G)[..., None]

    def step(S, inp):
        qn, un, wn, an, gn, kt, gl = inp
        U = un - jnp.einsum('bhck,bhkv->bhcv', wn, S)
        o = jnp.exp(gn)[..., None] * jnp.einsum('bhck,bhkv->bhcv', qn, S) + jnp.einsum('bhcs,bhsv->bhcv', an, U)
        S = jnp.exp(gl)[..., None, None] * S + jnp.einsum('bhck,bhcv->bhkv', kt, U)
        return S, o

    S_fin, o = lax.scan(step, S0, (qc, u_bar, w_k, a_qk, G, k_tail, g_last))
    o = jnp.moveaxis(o, (0, 2), (1, 3)).reshape(Bn, L, H, DV)
    return o, S_fin


def _stick_breaking(q, k, v, q_pos, k_pos):
    z = jnp.einsum('bqhd,bkhd->bhqk', q, k).astype(jnp.float32) * (B_DH ** -0.5)
    vis = k_pos[None, :] < q_pos[:, None]
    log_beta = jax.nn.log_sigmoid(z)
    log_stay = jnp.where(vis, jax.nn.log_sigmoid(-z), 0.0)
    between = lax.cumsum(log_stay, axis=3, reverse=True) - log_stay
    w = jnp.where(vis, jnp.exp(log_beta + between), 0.0)
    return jnp.einsum('bhqk,bkhd->bqhd', w, v.astype(jnp.float32))


def _sb_blocked(q, k, v, q_start):
    Bn, Tq, H, D = q.shape
    qb = min(Q_BLOCK, Tq)
    nb = Tq // qb
    k_pos = jnp.arange(k.shape[1])
    q_blocks = jnp.moveaxis(q.reshape(Bn, nb, qb, H, D), 1, 0)

    def one(args):
        bi, qi = args
        return _stick_breaking(qi, k, v, q_start + bi * qb + jnp.arange(qb), k_pos)

    o = lax.map(one, (jnp.arange(nb), q_blocks))
    return jnp.moveaxis(o, 0, 1).reshape(Bn, Tq, H, D)


def _hier_moe(x, w_group, b_group, w_er, b_er, w1, w3, w2):
    Bn, L, D = x.shape
    t = x.reshape(Bn * L, D)
    p_group = jax.nn.softmax((t @ w_group).astype(jnp.float32) + b_group, axis=-1)
    p_top, g_idx = lax.top_k(p_group, 1)
    g_hot = jax.nn.one_hot(g_idx[:, 0], N_GROUPS, dtype=jnp.float32)
    e_logit = jnp.einsum('td,gde->tge', t, w_er).astype(jnp.float32) + b_er
    e_sel = jnp.einsum('tg,tge->te', g_hot, e_logit)
    e_top, e_idx = lax.top_k(e_sel, TOP_K_IN_GROUP)
    e_w = jax.nn.softmax(e_top, axis=-1) * p_top
    e_gate = jnp.einsum('tk,tke->te', e_w, jax.nn.one_hot(e_idx, EXPERTS_PER_GROUP, dtype=jnp.float32))
    gates = g_hot[:, :, None] * e_gate[:, None, :]
    out = jnp.zeros((Bn * L, D), jnp.float32)
    for gi in range(N_GROUPS):
        hid = jax.nn.silu(jnp.einsum('td,edf->tef', t, w1[gi])) * jnp.einsum('td,edf->tef', t, w3[gi])
        out = out + jnp.einsum('tef,efd->td', hid * gates[:, gi, :, None].astype(hid.dtype), w2[gi])
    return out.reshape(Bn, L, D).astype(x.dtype)


def _layer(x, conv_hist, S0, k_past, v_past, lp):
    (norm1_g, w_in, conv_w, a_log, dt_bias, a_norm_g, q_norm_g, k_norm_g,
     w_br_a, w_br_b, w_out, norm2_g, w_group, b_group, w_er, b_er, w1, w3, w2) = lp
    Bn, L, _ = x.shape
    P = k_past.shape[1]
    xn = _rms(x, norm1_g)
    proj = jnp.einsum('bld,dc->blc', xn, w_in)
    conv_in, z_a, b_a, a_a, q_b, k_b, v_b, gate_a, gate_b = jnp.split(proj, _in_splits(), axis=-1)

    conv_out, new_conv = _causal_conv(conv_in, conv_hist, conv_w)
    q_a, k_a, v_a = jnp.split(conv_out, [A_QK, 2 * A_QK], axis=-1)
    q_a = _l2(q_a.reshape(Bn, L, A_HEADS, A_DK))
    k_a = _l2(k_a.reshape(Bn, L, A_HEADS, A_DK))
    v_a = v_a.reshape(Bn, L, A_HEADS, A_DV).astype(jnp.float32)
    beta = jax.nn.sigmoid(b_a.astype(jnp.float32))
    g = -jnp.exp(a_log.astype(jnp.float32)) * jax.nn.softplus(a_a.astype(jnp.float32) + dt_bias.astype(jnp.float32))
    o_a, S_new = _gated_delta(q_a, k_a, v_a, g, beta, S0.astype(jnp.float32))
    o_a = _rms(o_a, a_norm_g) * jax.nn.silu(z_a.astype(jnp.float32).reshape(Bn, L, A_HEADS, A_DV))
    o_a = o_a.reshape(Bn, L, A_V).astype(x.dtype)

    q_b = _rms(q_b.reshape(Bn, L, B_HEADS, B_DH), q_norm_g)
    k_b = _rms(k_b.reshape(Bn, L, B_HEADS, B_DH), k_norm_g)
    v_b = v_b.reshape(Bn, L, B_HEADS, B_DH)
    k_all = jnp.concatenate([k_past.astype(k_b.dtype), k_b], axis=1)
    v_all = jnp.concatenate([v_past.astype(v_b.dtype), v_b], axis=1)
    o_b = _sb_blocked(q_b, k_all, v_all, P).reshape(Bn, L, B_W).astype(x.dtype)

    merged = jax.nn.sigmoid(gate_a) * (o_a @ w_br_a) + jax.nn.sigmoid(gate_b) * (o_b @ w_br_b)
    h = x + (merged @ w_out).astype(x.dtype)
    y = h + _hier_moe(_rms(h, norm2_g), w_group, b_group, w_er, b_er, w1, w3, w2)
    return y, new_conv, S_new, k_b, v_b


def setup_inputs(seed: int = 0) -> dict:
    key = jax.random.key(seed)
    ks = jax.random.split(key, 28)

    def nrm(k, shape, s):
        return s * jax.random.normal(k, shape, jnp.float32)

    def gain(k, n):
        return 1.0 + nrm(k, (DEPTH, n), 0.02)

    dt = jnp.exp(jax.random.uniform(ks[9], (DEPTH, A_HEADS), jnp.float32, math.log(1e-3), math.log(1e-1)))
    return {
        'x_prompt': nrm(ks[0], (BATCH, SEQ, D_MODEL), 1.0),
        'x_sample': nrm(ks[1], (DEC_BATCH, DEC_SEQ, D_MODEL), 1.0),
        'cache_conv_a': nrm(ks[2], (DEPTH, DEC_BATCH, CONV_W - 1, A_CONV_DIM), 1.0),
        'state_delta_a': nrm(ks[3], (DEPTH, DEC_BATCH, A_HEADS, A_DK, A_DV), 0.5),
        'cache_k_sb': nrm(ks[4], (DEPTH, DEC_BATCH, PAST_LEN, B_HEADS, B_DH), 1.0),
        'cache_v_sb': nrm(ks[5], (DEPTH, DEC_BATCH, PAST_LEN, B_HEADS, B_DH), 1.0),
        'norm1_g': gain(ks[6], D_MODEL),
        'w_in': nrm(ks[7], (DEPTH, D_MODEL, IN_COLS), D_MODEL ** -0.5),
        'conv_a_w': nrm(ks[8], (DEPTH, CONV_W, A_CONV_DIM), CONV_W ** -0.5),
        'a_log': jnp.log(jax.random.uniform(ks[10], (DEPTH, A_HEADS), jnp.float32, 1.0, 16.0)),
        'dt_bias': dt + jnp.log(-jnp.expm1(-dt)),
        'a_out_norm_g': gain(ks[11], A_DV),
        'sb_q_norm_g': gain(ks[12], B_DH),
        'sb_k_norm_g': gain(ks[13], B_DH),
        'w_branch_a': nrm(ks[14], (DEPTH, A_V, D_MODEL), A_V ** -0.5),
        'w_branch_b': nrm(ks[15], (DEPTH, B_W, D_MODEL), B_W ** -0.5),
        'w_out': nrm(ks[16], (DEPTH, D_MODEL, D_MODEL), D_MODEL ** -0.5),
        'norm2_g': gain(ks[17], D_MODEL),
        'w_group': nrm(ks[18], (DEPTH, D_MODEL, N_GROUPS), D_MODEL ** -0.5),
        'b_group': nrm(ks[19], (DEPTH, N_GROUPS), 0.01),
        'w_expert_router': nrm(ks[20], (DEPTH, N_GROUPS, D_MODEL, EXPERTS_PER_GROUP), D_MODEL ** -0.5),
        'b_expert_router': nrm(ks[21], (DEPTH, N_GROUPS, EXPERTS_PER_GROUP), 0.01),
        'w1': nrm(ks[22], (DEPTH, N_GROUPS, EXPERTS_PER_GROUP, D_MODEL, D_EXPERT), D_MODEL ** -0.5),
        'w3': nrm(ks[23], (DEPTH, N_GROUPS, EXPERTS_PER_GROUP, D_MODEL, D_EXPERT), D_MODEL ** -0.5),
        'w2': nrm(ks[24], (DEPTH, N_GROUPS, EXPERTS_PER_GROUP, D_EXPERT, D_MODEL), D_EXPERT ** -0.5),
    }


def reference(x_prompt, x_sample, cache_conv_a, state_delta_a, cache_k_sb, cache_v_sb,
              norm1_g, w_in, conv_a_w, a_log, dt_bias, a_out_norm_g, sb_q_norm_g, sb_k_norm_g,
              w_branch_a, w_branch_b, w_out, norm2_g, w_group, b_group, w_expert_router,
              b_expert_router, w1, w3, w2):
    yp, ys = x_prompt, x_sample
    Bp = x_prompt.shape[0]
    conv_p, delta_p, k_p, v_p = [], [], [], []
    conv_s, delta_s, k_s, v_s = [], [], [], []
    for l in range(DEPTH):
        lp = (norm1_g[l], w_in[l], conv_a_w[l], a_log[l], dt_bias[l], a_out_norm_g[l], sb_q_norm_g[l],
              sb_k_norm_g[l], w_branch_a[l], w_branch_b[l], w_out[l], norm2_g[l], w_group[l], b_group[l],
              w_expert_router[l], b_expert_router[l], w1[l], w3[l], w2[l])
        zero_conv = jnp.zeros((Bp, CONV_W - 1, A_CONV_DIM), x_prompt.dtype)
        zero_S = jnp.zeros((Bp, A_HEADS, A_DK, A_DV), jnp.float32)
        no_kv = jnp.zeros((Bp, 0, B_HEADS, B_DH), x_prompt.dtype)
        yp, c, S, kr, vr = _layer(yp, zero_conv, zero_S, no_kv, no_kv, lp)
        conv_p.append(c)
        delta_p.append(S)
        k_p.append(kr)
        v_p.append(vr)
        ys, c, S, kr, vr = _layer(ys, cache_conv_a[l], state_delta_a[l], cache_k_sb[l], cache_v_sb[l], lp)
        conv_s.append(c)
        delta_s.append(S)
        k_s.append(kr)
        v_s.append(vr)
    return (yp, ys, jnp.stack(conv_p), jnp.stack(delta_p), jnp.stack(k_p), jnp.stack(v_p),
            jnp.stack(conv_s), jnp.stack(delta_s), jnp.stack(k_s), jnp.stack(v_s))
```

```python
import functools

import jax
import jax.numpy as jnp
from jax import lax
from jax.experimental import pallas as pl
from jax.experimental.pallas import tpu as pltpu

F32 = jnp.float32
BF16 = jnp.bfloat16

EPS = 1e-6
CHUNK = 64
HEAD_DIM = 128
N_HEADS = 8
D_MODEL = 1024
CONV_W = 4
N_GROUPS = 4
EXPERTS_PER_GROUP = 8
N_EXPERTS = N_GROUPS * EXPERTS_PER_GROUP
D_EXPERT = 512
LANES = 128
SUBLANES = 8
VMEM_LIMIT_BYTES = 56 * 1024 * 1024

PROJ_TM = 512
PROJ_TN = 512
MERGE_TM = 256
MOE_TM = 256
COMBINE_TM = 256
DELTA_ROWS = 2 * CHUNK


def _dot(a, b):
    return jnp.dot(a, b, preferred_element_type=F32)


def _dot_nt(a, b):
    return lax.dot_general(a, b, (((1,), (1,)), ((), ())), preferred_element_type=F32)


def _split_bf16(x):
    hi = x.astype(BF16)
    lo = (x - hi.astype(F32)).astype(BF16)
    return hi, lo


def _sigmoid(x):
    return 1.0 / (1.0 + jnp.exp(-x))


def _softplus(x):
    return jnp.maximum(x, 0.0) + jnp.log(1.0 + jnp.exp(-jnp.abs(x)))


def _rms_rows(x, g):
    ms = jnp.mean(x * x, axis=-1, keepdims=True)
    return x * lax.rsqrt(ms + EPS) * g


_SEG_CONV, _SEG_Z, _SEG_Q, _SEG_K, _SEG_V, _SEG_G, _SEG_END = 0, 6, 8, 10, 12, 14, 18


def _proj_kernel(xp_ref, xs_ref, g1_ref, w_ref, wba_ref, qg_ref, kg_ref, alog_ref, dtb_ref,
                 conv_ref, z_ref, qb_ref, kp_ref, ks_ref, vp_ref, vs_ref, gate_ref, ba_ref,
                 xn_ref, *, n_prompt_tiles):
    i = pl.program_id(0)
    j = pl.program_id(1)
    is_prompt = i < n_prompt_tiles

    def norm_from(x_ref):
        xn_ref[...] = _rms_rows(x_ref[...], g1_ref[...]).astype(BF16)

    @pl.when(j == 0)
    def _():
        pl.when(is_prompt)(lambda: norm_from(xp_ref))
        pl.when(jnp.logical_not(is_prompt))(lambda: norm_from(xs_ref))
        ba = _dot(xn_ref[...], wba_ref[...])
        lane = lax.broadcasted_iota(jnp.int32, ba.shape, 1)
        beta = _sigmoid(ba)
        g = -jnp.exp(alog_ref[...]) * _softplus(ba + dtb_ref[...])
        ba_ref[...] = jnp.where(lane < N_HEADS, beta, g)

    def acc():
        return _dot(xn_ref[...], w_ref[...])

    def head_rms_store(out_ref, val, g):
        for h in range(PROJ_TN // HEAD_DIM):
            sl = slice(h * HEAD_DIM, (h + 1) * HEAD_DIM)
            out_ref[:, sl] = _rms_rows(val[:, sl], g)

    @pl.when(j < _SEG_Z)
    def _():
        conv_ref[...] = acc()

    @pl.when(jnp.logical_and(j >= _SEG_Z, j < _SEG_Q))
    def _():
        z_ref[...] = acc()

    @pl.when(jnp.logical_and(j >= _SEG_Q, j < _SEG_K))
    def _():
        head_rms_store(qb_ref, acc(), qg_ref[...])

    @pl.when(jnp.logical_and(j >= _SEG_K, j < _SEG_V))
    def _():
        val = acc()
        pl.when(is_prompt)(lambda: head_rms_store(kp_ref, val, kg_ref[...]))
        pl.when(jnp.logical_not(is_prompt))(lambda: head_rms_store(ks_ref, val, kg_ref[...]))

    @pl.when(jnp.logical_and(j >= _SEG_V, j < _SEG_G))
    def _():
        val = acc()

        def st(ref):
            ref[...] = val
        pl.when(is_prompt)(lambda: st(vp_ref))
        pl.when(jnp.logical_not(is_prompt))(lambda: st(vs_ref))

    @pl.when(j >= _SEG_G)
    def _():
        gate_ref[...] = _sigmoid(acc())


def _proj(x_p, x_s, g1, w_main, w_ba, qg, kg, alog_pad, dtb_pad):
    tp, d = x_p.shape
    ts = x_s.shape[0]
    tm, tn = PROJ_TM, PROJ_TN
    npt, nst = tp // tm, ts // tm
    n_tiles = npt + nst
    t_all = tp + ts

    def clampj(j, lo, n):
        return jnp.clip(j - lo, 0, n - 1)

    def p_rows(i):
        return jnp.minimum(i, npt - 1)

    def s_rows(i):
        return jnp.maximum(i - npt, 0)

    def p_cols(i, j, lo, n):
        return jnp.where(i < npt, clampj(j, lo, n), n - 1)

    def s_cols(i, j, lo, n):
        return jnp.where(i < npt, 0, clampj(j, lo, n))

    const = lambda i, j: (0, 0)
    in_specs = [
        pl.BlockSpec((tm, d), lambda i, j: (p_rows(i), 0)),
        pl.BlockSpec((tm, d), lambda i, j: (s_rows(i), 0)),
        pl.BlockSpec((1, d), const),
        pl.BlockSpec((d, tn), lambda i, j: (0, j)),
        pl.BlockSpec((d, LANES), const),
        pl.BlockSpec((1, HEAD_DIM), const),
        pl.BlockSpec((1, HEAD_DIM), const),
        pl.BlockSpec((1, LANES), const),
        pl.BlockSpec((1, LANES), const),
    ]
    out_specs = [
        pl.BlockSpec((tm, tn), lambda i, j: (i, clampj(j, _SEG_CONV, 6))),
        pl.BlockSpec((tm, tn), lambda i, j: (i, clampj(j, _SEG_Z, 2))),
        pl.BlockSpec((tm, tn), lambda i, j: (i, clampj(j, _SEG_Q, 2))),
        pl.BlockSpec((tm, tn), lambda i, j: (p_rows(i), p_cols(i, j, _SEG_K, 2))),
        pl.BlockSpec((tm, tn), lambda i, j: (s_rows(i), s_cols(i, j, _SEG_K, 2))),
        pl.BlockSpec((tm, tn), lambda i, j: (p_rows(i), p_cols(i, j, _SEG_V, 2))),
        pl.BlockSpec((tm, tn), lambda i, j: (s_rows(i), s_cols(i, j, _SEG_V, 2))),
        pl.BlockSpec((tm, tn), lambda i, j: (i, clampj(j, _SEG_G, 4))),
        pl.BlockSpec((tm, LANES), lambda i, j: (i, 0)),
    ]
    out_shape = [
        jax.ShapeDtypeStruct((t_all, 3 * D_MODEL), F32),
        jax.ShapeDtypeStruct((t_all, D_MODEL), F32),
        jax.ShapeDtypeStruct((t_all, D_MODEL), F32),
        jax.ShapeDtypeStruct((tp, D_MODEL), F32),
        jax.ShapeDtypeStruct((ts, D_MODEL), F32),
        jax.ShapeDtypeStruct((tp, D_MODEL), F32),
        jax.ShapeDtypeStruct((ts, D_MODEL), F32),
        jax.ShapeDtypeStruct((t_all, 2 * D_MODEL), F32),
        jax.ShapeDtypeStruct((t_all, LANES), F32),
    ]
    return pl.pallas_call(
        functools.partial(_proj_kernel, n_prompt_tiles=npt),
        grid=(n_tiles, _SEG_END),
        in_specs=in_specs,
        out_specs=out_specs,
        out_shape=out_shape,
        scratch_shapes=[pltpu.VMEM((tm, d), BF16)],
        compiler_params=pltpu.CompilerParams(
            dimension_semantics=("arbitrary", "arbitrary"),
            vmem_limit_bytes=VMEM_LIMIT_BYTES),
    )(x_p, x_s, g1, w_main, w_ba, qg, kg, alog_pad, dtb_pad)


def _conv_silu(cur, prev8, w):
    acc = cur * w[CONV_W - 1:CONV_W, :]
    rows8 = lax.broadcasted_iota(jnp.int32, (SUBLANES, LANES), 0)
    for k in range(1, CONV_W):
        sh = pltpu.roll(cur, k, axis=0)
        ph = pltpu.roll(prev8, k, axis=0)
        head = jnp.where(rows8 < k, ph, sh[0:SUBLANES])
        shifted = jnp.concatenate([head, sh[SUBLANES:]], axis=0)
        acc = acc + shifted * w[CONV_W - 1 - k:CONV_W - k, :]
    return acc * _sigmoid(acc)


def _l2_rows(x):
    return x * lax.rsqrt(jnp.sum(x * x, axis=-1, keepdims=True) + EPS)


def _delta_kernel(qc_ref, kc_ref, vc_ref, hq_ref, hk_ref, hv_ref, wq_ref, wk_ref, wv_ref,
                  ba_ref, z_ref, s0_ref, ng_ref, o_ref, sout_ref,
                  s_ref, tq_ref, tk_ref, tv_ref, *, lt, valid_chunks):
    h = pl.program_id(1)
    tt = pl.program_id(2)
    R = DELTA_ROWS

    @pl.when(tt == 0)
    def _():
        s_ref[...] = s0_ref[0, 0]
        tq_ref[...] = hq_ref[0]
        tk_ref[...] = hk_ref[0]
        tv_ref[...] = hv_ref[0]

    q_raw, k_raw, v_raw = qc_ref[...], kc_ref[...], vc_ref[...]
    q_all = _l2_rows(_conv_silu(q_raw, tq_ref[...], wq_ref[...])) * (HEAD_DIM ** -0.5)
    k_all = _l2_rows(_conv_silu(k_raw, tk_ref[...], wk_ref[...]))
    v_all = _conv_silu(v_raw, tv_ref[...], wv_ref[...])
    tq_ref[...] = q_raw[lt - SUBLANES:, :]
    tk_ref[...] = k_raw[lt - SUBLANES:, :]
    tv_ref[...] = v_raw[lt - SUBLANES:, :]

    ba = ba_ref[...]
    lane = lax.broadcasted_iota(jnp.int32, ba.shape, 1)
    beta_col = jnp.sum(jnp.where(lane == h, ba, 0.0), axis=1, keepdims=True)
    g_col = jnp.sum(jnp.where(lane == h + N_HEADS, ba, 0.0), axis=1, keepdims=True)

    row = lax.broadcasted_iota(jnp.int32, (R, R), 0)
    col = lax.broadcasted_iota(jnp.int32, (R, R), 1)
    log2 = lambda n: n.bit_length() - 1
    same_chunk = (row >> log2(CHUNK)) == (col >> log2(CHUNK))
    incl = jnp.logical_and(same_chunk, col <= row)
    strict = jnp.logical_and(same_chunk, col < row)
    cs_mat = jnp.where(incl, 1.0, 0.0).astype(BF16)
    eye = jnp.where(row == col, 1.0, 0.0)

    def level_mask(s):
        same = (row >> log2(2 * s)) == (col >> log2(2 * s))
        return jnp.logical_and(jnp.logical_and(same, ((row >> log2(s)) & 1) == 1), ((col >> log2(s)) & 1) == 0)

    zeros_half = jnp.zeros((CHUNK, HEAD_DIM), F32)
    o_parts = []
    S = s_ref[...]
    for d in range(lt // R):
        rs = slice(d * R, (d + 1) * R)
        Q, K, V = q_all[rs], k_all[rs], v_all[rs]
        beta_b = jnp.broadcast_to(beta_col[rs], (R, R))
        g_b = jnp.broadcast_to(g_col[rs], (R, R))
        g_hi, g_lo = _split_bf16(g_b)
        Gb = _dot(cs_mat, g_hi) + _dot(cs_mat, g_lo)
        diff = Gb - Gb.T
        decay = jnp.where(incl, jnp.exp(jnp.where(incl, diff, 0.0)), 0.0)
        Kb = K.astype(BF16)
        kk = _dot_nt(Kb, Kb)
        qk = _dot_nt(Q.astype(BF16), Kb)
        Lm = jnp.where(strict, beta_b * kk * decay, 0.0)
        a_qk = qk * decay
        T = eye - jnp.where(level_mask(1), Lm, 0.0)
        s = 2
        while s < CHUNK:
            A = jnp.where(level_mask(s), Lm, 0.0).astype(BF16)
            Tb = T.astype(BF16)
            T = T - _dot(Tb, _dot(A, Tb).astype(BF16))
            s *= 2
        Tb = T.astype(BF16)
        expG = jnp.exp(Gb)
        u_bar = _dot(Tb, (beta_b * V).astype(BF16))
        w_k = _dot(Tb, (beta_b * expG * K).astype(BF16))
        g_last = jnp.concatenate(
            [jnp.broadcast_to(Gb[CHUNK - 1:CHUNK, :], (CHUNK, R)),
             jnp.broadcast_to(Gb[R - 1:R, :], (CHUNK, R))], axis=0)
        k_tail_t = (K * jnp.exp(g_last - Gb)).T.astype(BF16)
        n_valid = min(max(valid_chunks - 2 * d, 0), 2)
        for c in range(n_valid):
            rc = slice(c * CHUNK, (c + 1) * CHUNK)
            Sb = S.astype(BF16)
            U = u_bar[rc] - _dot(w_k[rc].astype(BF16), Sb)
            Ucat = (jnp.concatenate([U, zeros_half], axis=0) if c == 0
                    else jnp.concatenate([zeros_half, U], axis=0)).astype(BF16)
            o_parts.append(expG[rc] * _dot(Q[rc].astype(BF16), Sb) + _dot(a_qk[rc].astype(BF16), Ucat))
            S = jnp.exp(g_last[rc][0:1, :]) * S + _dot(k_tail_t, Ucat)
        for c in range(n_valid, 2):
            o_parts.append(zeros_half)
    s_ref[...] = S
    o_raw = jnp.concatenate(o_parts, axis=0)
    z = z_ref[...]
    o_ref[...] = _rms_rows(o_raw, ng_ref[...]) * (z * _sigmoid(z))

    @pl.when(tt == pl.num_programs(2) - 1)
    def _():
        sout_ref[0, 0] = S


def _delta(conv_in, row_block_off, hist8, conv_w, ba, z, s0, norm_g, *, batch, length, lt, valid_chunks):
    nt = length // lt
    nh = N_HEADS

    def rows(b, h, t):
        return row_block_off + b * nt + t

    def tok(col_off):
        return pl.BlockSpec((lt, HEAD_DIM), lambda b, h, t: (rows(b, h, t), col_off + h))

    def hist(col_off):
        return pl.BlockSpec((1, SUBLANES, HEAD_DIM), lambda b, h, t: (b, 0, col_off + h))

    def cw(col_off):
        return pl.BlockSpec((CONV_W, HEAD_DIM), lambda b, h, t: (0, col_off + h))

    in_specs = [tok(0), tok(nh), tok(2 * nh), hist(0), hist(nh), hist(2 * nh), cw(0), cw(nh), cw(2 * nh),
                pl.BlockSpec((lt, LANES), lambda b, h, t: (rows(b, h, t), 0)),
                tok(0),
                pl.BlockSpec((1, 1, HEAD_DIM, HEAD_DIM), lambda b, h, t: (b, h, 0, 0)),
                pl.BlockSpec((1, HEAD_DIM), lambda b, h, t: (0, 0))]
    out_specs = [pl.BlockSpec((lt, HEAD_DIM), lambda b, h, t: (b * nt + t, h)),
                 pl.BlockSpec((1, 1, HEAD_DIM, HEAD_DIM), lambda b, h, t: (b, h, 0, 0))]
    out_shape = [jax.ShapeDtypeStruct((batch * length, nh * HEAD_DIM), F32),
                 jax.ShapeDtypeStruct((batch, nh, HEAD_DIM, HEAD_DIM), F32)]
    return pl.pallas_call(
        functools.partial(_delta_kernel, lt=lt, valid_chunks=valid_chunks),
        grid=(batch, nh, nt),
        in_specs=in_specs,
        out_specs=out_specs,
        out_shape=out_shape,
        scratch_shapes=[pltpu.VMEM((HEAD_DIM, HEAD_DIM), F32),
                        pltpu.VMEM((SUBLANES, HEAD_DIM), F32),
                        pltpu.VMEM((SUBLANES, HEAD_DIM), F32),
                        pltpu.VMEM((SUBLANES, HEAD_DIM), F32)],
        compiler_params=pltpu.CompilerParams(
            dimension_semantics=("arbitrary", "arbitrary", "arbitrary"),
            vmem_limit_bytes=VMEM_LIMIT_BYTES),
    )(conv_in, conv_in, conv_in, hist8, hist8, hist8, conv_w, conv_w, conv_w, ba, z, s0, norm_g)


def _sb_kernel(q_ref, k_ref, v_ref, o_ref, *, tq, bk, q_start):
    i = pl.program_id(2)
    q = (q_ref[...] * (HEAD_DIM ** -0.5)).astype(BF16)
    q_pos0 = q_start + i * tq
    n_kb = (q_pos0 + tq - 1 + bk - 1) // bk
    rr = lax.broadcasted_iota(jnp.int32, (bk, bk), 0)
    cc = lax.broadcasted_iota(jnp.int32, (bk, bk), 1)
    later = jnp.where(rr > cc, 1.0, 0.0).astype(BF16)
    q_pos = q_pos0 + lax.broadcasted_iota(jnp.int32, (tq, bk), 0)
    k_off = lax.broadcasted_iota(jnp.int32, (tq, bk), 1)

    def body(step, carry):
        o, acc = carry
        k0 = pl.multiple_of((n_kb - 1 - step) * bk, bk)
        kb = k_ref[0, pl.ds(k0, bk), :].astype(BF16)
        vb = v_ref[0, pl.ds(k0, bk), :].astype(BF16)
        zz = _dot_nt(q, kb)
        vis = (k0 + k_off) < q_pos
        t = jnp.log(1.0 + jnp.exp(-jnp.abs(zz)))
        log_beta = jnp.minimum(zz, 0.0) - t
        log_stay = jnp.where(vis, -jnp.maximum(zz, 0.0) - t, 0.0)
        hi, lo = _split_bf16(log_stay)
        between = _dot(hi, later) + _dot(lo, later)
        w = jnp.where(vis, jnp.exp(log_beta + between + acc), 0.0)
        o = o + _dot(w.astype(BF16), vb)
        acc = acc + jnp.sum(log_stay, axis=1, keepdims=True)
        return o, acc

    o, _ = lax.fori_loop(0, n_kb, body, (jnp.zeros((tq, HEAD_DIM), F32), jnp.zeros((tq, 1), F32)))
    o_ref[...] = o


def _sb_attn(q, q_row_block_off, k, v, *, batch, q_len, tq, bk, q_start):
    nq = q_len // tq
    lk = k.shape[1]
    kv_spec = pl.BlockSpec((1, lk, HEAD_DIM), lambda b, h, i: (b, 0, h))
    return pl.pallas_call(
        functools.partial(_sb_kernel, tq=tq, bk=bk, q_start=q_start),
        grid=(batch, N_HEADS, nq),
        in_specs=[pl.BlockSpec((tq, HEAD_DIM), lambda b, h, i: (q_row_block_off + b * nq + i, h)),
                  kv_spec, kv_spec],
        out_specs=pl.BlockSpec((tq, HEAD_DIM), lambda b, h, i: (b * nq + i, h)),
        out_shape=jax.ShapeDtypeStruct((batch * q_len, N_HEADS * HEAD_DIM), F32),
        compiler_params=pltpu.CompilerParams(
            dimension_semantics=("arbitrary", "arbitrary", "arbitrary"),
            vmem_limit_bytes=VMEM_LIMIT_BYTES),
    )(q, k, v)


def _merge_kernel(oap_ref, oas_ref, obp_ref, obs_ref, gate_ref, xp_ref, xs_ref,
                  wa_ref, wb_ref, wo_ref, g2_ref, wrh_ref, wrl_ref, br_ref,
                  h_ref, hn_ref, route_ref, *, n_prompt_tiles):
    i = pl.program_id(0)

    def run(oa_ref, ob_ref, x_ref):
        ga = gate_ref[:, :D_MODEL]
        gb = gate_ref[:, D_MODEL:]
        merged = ga * _dot(oa_ref[...].astype(BF16), wa_ref[...]) + gb * _dot(ob_ref[...].astype(BF16), wb_ref[...])
        hh = x_ref[...] + _dot(merged.astype(BF16), wo_ref[...])
        h_ref[...] = hh
        hn = _rms_rows(hh, g2_ref[...])
        hn_ref[...] = hn
        hi, lo = _split_bf16(hn)
        logits = _dot(hi, wrh_ref[...]) + _dot(lo, wrh_ref[...]) + _dot(hi, wrl_ref[...]) + br_ref[...]
        lane_i = lax.broadcasted_iota(jnp.int32, logits.shape, 1)
        lane = lane_i.astype(F32)
        neg = jnp.float32(-jnp.inf)
        big = jnp.float32(LANES)
        is_g = lane_i < N_GROUPS
        lg = jnp.where(is_g, logits, neg)
        g_max = jnp.max(lg, axis=1, keepdims=True)
        g_idx = jnp.min(jnp.where(lg == g_max, lane, big), axis=1, keepdims=True)
        p_top = 1.0 / jnp.sum(jnp.where(is_g, jnp.exp(lg - g_max), 0.0), axis=1, keepdims=True)
        e_lane = lane_i - N_GROUPS
        lane_group = (e_lane >> (EXPERTS_PER_GROUP.bit_length() - 1)).astype(F32)
        sel = jnp.logical_and(jnp.logical_and(e_lane >= 0, e_lane < N_EXPERTS), lane_group == g_idx)
        le = jnp.where(sel, logits, neg)
        m1 = jnp.max(le, axis=1, keepdims=True)
        i1 = jnp.min(jnp.where(le == m1, lane, big), axis=1, keepdims=True)
        le2 = jnp.where(lane == i1, neg, le)
        m2 = jnp.max(le2, axis=1, keepdims=True)
        i2 = jnp.min(jnp.where(le2 == m2, lane, big), axis=1, keepdims=True)
        e2 = jnp.exp(m2 - m1)
        w1 = p_top / (1.0 + e2)
        w2 = p_top * e2 / (1.0 + e2)
        route = jnp.where(lane_i == 0, i1 - N_GROUPS,
                          jnp.where(lane_i == 1, i2 - N_GROUPS,
                                    jnp.where(lane_i == 2, w1, jnp.where(lane_i == 3, w2, 0.0))))
        route_ref[...] = route

    pl.when(i < n_prompt_tiles)(lambda: run(oap_ref, obp_ref, xp_ref))
    pl.when(i >= n_prompt_tiles)(lambda: run(oas_ref, obs_ref, xs_ref))


def _merge(oa_p, oa_s, ob_p, ob_s, gates, x_p, x_s, wa, wb, wo, g2, wr_hi, wr_lo, br):
    tp, d = x_p.shape
    ts = x_s.shape[0]
    tm = MERGE_TM
    npt = tp // tm
    n_tiles = npt + ts // tm
    t_all = tp + ts
    p_spec = pl.BlockSpec((tm, d), lambda i: (jnp.minimum(i, npt - 1), 0))
    s_spec = pl.BlockSpec((tm, d), lambda i: (jnp.maximum(i - npt, 0), 0))
    const = lambda i: (0, 0)
    w_spec = pl.BlockSpec((d, d), const)
    in_specs = [p_spec, s_spec, p_spec, s_spec,
                pl.BlockSpec((tm, 2 * d), lambda i: (i, 0)),
                p_spec, s_spec, w_spec, w_spec, w_spec,
                pl.BlockSpec((1, d), const),
                pl.BlockSpec((d, LANES), const), pl.BlockSpec((d, LANES), const),
                pl.BlockSpec((1, LANES), const)]
    out_specs = [pl.BlockSpec((tm, d), lambda i: (i, 0)),
                 pl.BlockSpec((tm, d), lambda i: (i, 0)),
                 pl.BlockSpec((tm, LANES), lambda i: (i, 0))]
    out_shape = [jax.ShapeDtypeStruct((t_all, d), F32),
                 jax.ShapeDtypeStruct((t_all, d), F32),
                 jax.ShapeDtypeStruct((t_all, LANES), F32)]
    return pl.pallas_call(
        functools.partial(_merge_kernel, n_prompt_tiles=npt),
        grid=(n_tiles,),
        in_specs=in_specs,
        out_specs=out_specs,
        out_shape=out_shape,
        compiler_params=pltpu.CompilerParams(
            dimension_semantics=("arbitrary",),
            vmem_limit_bytes=VMEM_LIMIT_BYTES),
    )(oa_p, oa_s, ob_p, ob_s, gates, x_p, x_s, wa, wb, wo, g2, wr_hi, wr_lo, br)


def _moe_kernel(te_ref, src_ref, nused_ref, hn_hbm, w1_ref, w3_ref, w2_ref, y_ref,
                xbuf, w1b, w3b, w2b, sem, *, tm):
    i = pl.program_id(0)
    n_used = nused_ref[0]
    slot = i % 2

    def row_copy(tile, sl, r):
        tok = src_ref[tile * tm + r]
        return pltpu.make_async_copy(hn_hbm.at[pl.ds(tok, 1), :], xbuf.at[sl, pl.ds(r, 1), :], sem.at[sl])

    def start_gather(tile, sl):
        def issue(r, c):
            row_copy(tile, sl, r).start()
            return c
        lax.fori_loop(0, tm, issue, 0)

    def wait_gather(sl):
        pltpu.make_async_copy(hn_hbm.at[pl.ds(0, tm), :], xbuf.at[sl], sem.at[sl]).wait()

    @pl.when(jnp.logical_and(i == 0, n_used > 0))
    def _():
        start_gather(0, 0)

    @pl.when(i + 1 < n_used)
    def _():
        start_gather(i + 1, 1 - slot)

    @pl.when(i < n_used)
    def _():
        prev = te_ref[jnp.maximum(i - 1, 0)]
        fresh = jnp.logical_or(i == 0, te_ref[i] != prev)

        @pl.when(fresh)
        def _():
            w1b[...] = w1_ref[0].astype(BF16)
            w3b[...] = w3_ref[0].astype(BF16)
            w2b[...] = w2_ref[0].astype(BF16)

        wait_gather(slot)
        x = xbuf[slot].astype(BF16)
        a = _dot(x, w1b[...])
        hid = (a * _sigmoid(a)) * _dot(x, w3b[...])
        y_ref[...] = _dot(hid.astype(BF16), w2b[...])

    @pl.when(i >= n_used)
    def _():
        y_ref[...] = jnp.zeros_like(y_ref)


def _moe_ffn(tile_expert, src, n_used, hn, w1, w3, w2, *, n_tiles):
    tm = MOE_TM
    d = hn.shape[1]
    f = w1.shape[2]
    grid_spec = pltpu.PrefetchScalarGridSpec(
        num_scalar_prefetch=3,
        grid=(n_tiles,),
        in_specs=[pl.BlockSpec(memory_space=pl.ANY),
                  pl.BlockSpec((1, d, f), lambda i, te, src, nu: (te[i], 0, 0)),
                  pl.BlockSpec((1, d, f), lambda i, te, src, nu: (te[i], 0, 0)),
                  pl.BlockSpec((1, f, d), lambda i, te, src, nu: (te[i], 0, 0))],
        out_specs=pl.BlockSpec((tm, d), lambda i, te, src, nu: (i, 0)),
        scratch_shapes=[pltpu.VMEM((2, tm, d), F32),
                        pltpu.VMEM((d, f), BF16), pltpu.VMEM((d, f), BF16), pltpu.VMEM((f, d), BF16),
                        pltpu.SemaphoreType.DMA((2,))])
    return pl.pallas_call(
        functools.partial(_moe_kernel, tm=tm),
        grid_spec=grid_spec,
        out_shape=jax.ShapeDtypeStruct((n_tiles * tm, d), F32),
        compiler_params=pltpu.CompilerParams(
            dimension_semantics=("arbitrary",),
            vmem_limit_bytes=VMEM_LIMIT_BYTES),
    )(tile_expert, src, n_used, hn, w1, w3, w2)


def _combine_kernel(dest_ref, ys_hbm, h_ref, route_ref, yp_ref, ys_ref, buf, sem, *, tm, t_all, n_prompt_tiles):
    i = pl.program_id(0)
    n = pl.num_programs(0)
    slot = i % 2

    def start_gather(tile, sl):
        def issue(r, c):
            for k in range(2):
                row = dest_ref[k * t_all + tile * tm + r]
                pltpu.make_async_copy(ys_hbm.at[pl.ds(row, 1), :], buf.at[sl, k, pl.ds(r, 1), :],
                                      sem.at[sl]).start()
            return c
        lax.fori_loop(0, tm, issue, 0)

    def wait_gather(sl):
        for k in range(2):
            pltpu.make_async_copy(ys_hbm.at[pl.ds(0, tm), :], buf.at[sl, k], sem.at[sl]).wait()

    @pl.when(i == 0)
    def _():
        start_gather(0, 0)

    @pl.when(i + 1 < n)
    def _():
        start_gather(i + 1, 1 - slot)

    wait_gather(slot)
    route = route_ref[...]
    y = h_ref[...] + route[:, 2:3] * buf[slot, 0] + route[:, 3:4] * buf[slot, 1]

    def st(ref):
        ref[...] = y
    pl.when(i < n_prompt_tiles)(lambda: st(yp_ref))
    pl.when(i >= n_prompt_tiles)(lambda: st(ys_ref))


def _combine(dest, y_sorted, h, route, *, tp, ts):
    tm = COMBINE_TM
    d = h.shape[1]
    t_all = tp + ts
    npt = tp // tm
    n_tiles = t_all // tm
    grid_spec = pltpu.PrefetchScalarGridSpec(
        num_scalar_prefetch=1,
        grid=(n_tiles,),
        in_specs=[pl.BlockSpec(memory_space=pl.ANY),
                  pl.BlockSpec((tm, d), lambda i, de: (i, 0)),
                  pl.BlockSpec((tm, LANES), lambda i, de: (i, 0))],
        out_specs=[pl.BlockSpec((tm, d), lambda i, de: (jnp.minimum(i, npt - 1), 0)),
                   pl.BlockSpec((tm, d), lambda i, de: (jnp.maximum(i - npt, 0), 0))],
        scratch_shapes=[pltpu.VMEM((2, 2, tm, d), F32), pltpu.SemaphoreType.DMA((2,))])
    return pl.pallas_call(
        functools.partial(_combine_kernel, tm=tm, t_all=t_all, n_prompt_tiles=npt),
        grid_spec=grid_spec,
        out_shape=[jax.ShapeDtypeStruct((tp, d), F32), jax.ShapeDtypeStruct((ts, d), F32)],
        compiler_params=pltpu.CompilerParams(
            dimension_semantics=("arbitrary",),
            vmem_limit_bytes=VMEM_LIMIT_BYTES),
    )(dest, y_sorted, h, route)


def _routing_tables(route, *, tm, n_tiles):
    t_all = route.shape[0]
    e_flat = jnp.concatenate([route[:, 0], route[:, 1]]).astype(jnp.int32)
    onehot = (e_flat[:, None] == jnp.arange(N_EXPERTS, dtype=jnp.int32)[None, :]).astype(jnp.int32)
    csum = jnp.cumsum(onehot, axis=0)
    rank = jnp.sum(csum * onehot, axis=1) - 1
    counts = csum[-1]
    tiles_e = (counts + tm - 1) // tm
    tile_end = jnp.cumsum(tiles_e)
    tile_start = tile_end - tiles_e
    dest = (tile_start * tm)[e_flat] + rank
    token = jnp.arange(2 * t_all, dtype=jnp.int32) % t_all
    src = jnp.zeros((n_tiles * tm,), jnp.int32).at[dest].set(token, unique_indices=True)
    tile_expert = jnp.minimum(
        jnp.sum((jnp.arange(n_tiles, dtype=jnp.int32)[:, None] >= tile_end[None, :]).astype(jnp.int32), axis=1),
        N_EXPERTS - 1).astype(jnp.int32)
    n_used = tile_end[-1:].astype(jnp.int32)
    return tile_expert, src, n_used, dest.astype(jnp.int32)


def _pad_lanes(v, n=LANES, offset=0):
    out = jnp.zeros((1, n), F32)
    return out.at[0, offset:offset + v.shape[0]].set(v.astype(F32))


def kernel(x_prompt, x_sample, cache_conv_a, state_delta_a, cache_k_sb, cache_v_sb, norm1_g, w_in, conv_a_w, a_log, dt_bias, a_out_norm_g, sb_q_norm_g, sb_k_norm_g, w_branch_a, w_branch_b, w_out, norm2_g, w_group, b_group, w_expert_router, b_expert_router, w1, w3, w2):
    assert norm1_g.shape[0] == 1, "single-layer model"
    bp, lp, d = x_prompt.shape
    bs, ls, _ = x_sample.shape
    past = cache_k_sb.shape[2]
    tp, ts = bp * lp, bs * ls
    nh, hd = N_HEADS, HEAD_DIM
    a_conv = 3 * nh * hd

    wi = w_in[0]
    c_z = a_conv + nh * hd
    c_ba = c_z + 2 * nh
    w_main = jnp.concatenate([wi[:, :c_z], wi[:, c_ba:]], axis=1).astype(BF16)
    w_ba = jnp.pad(wi[:, c_z:c_ba], ((0, 0), (0, LANES - 2 * nh))).astype(BF16)
    alog_pad = _pad_lanes(a_log[0], offset=nh)
    dtb_pad = _pad_lanes(dt_bias[0], offset=nh)
    x_p = x_prompt.reshape(tp, d)
    x_s = x_sample.reshape(ts, d)

    conv_in, z_a, q_b, k_p, k_s, v_p, v_s, gates, ba = _proj(
        x_p, x_s, norm1_g, w_main, w_ba, sb_q_norm_g, sb_k_norm_g, alog_pad, dtb_pad)

    conv_w = conv_a_w[0]
    lt_p = 512
    hist_p = jnp.zeros((bp, SUBLANES, a_conv), F32)
    s0_p = jnp.zeros((bp, nh, hd, hd), F32)
    oa_p, delta_p = _delta(conv_in, 0, hist_p, conv_w, ba, z_a, s0_p, a_out_norm_g,
                           batch=bp, length=lp, lt=lt_p, valid_chunks=lt_p // CHUNK)
    pad_rows = DELTA_ROWS - ls

    def pad_stream(a):
        a = a[tp:].reshape(bs, ls, a.shape[-1])
        return jnp.pad(a, ((0, 0), (0, pad_rows), (0, 0))).reshape(bs * DELTA_ROWS, a.shape[-1])

    hist_s = jnp.pad(cache_conv_a[0], ((0, 0), (SUBLANES - (CONV_W - 1), 0), (0, 0)))
    oa_s_pad, delta_s = _delta(pad_stream(conv_in), 0, hist_s, conv_w, pad_stream(ba), pad_stream(z_a),
                               state_delta_a[0], a_out_norm_g,
                               batch=bs, length=DELTA_ROWS, lt=DELTA_ROWS, valid_chunks=ls // CHUNK)
    oa_s = oa_s_pad.reshape(bs, DELTA_ROWS, nh * hd)[:, :ls].reshape(ts, nh * hd)

    bk = 256
    ob_p = _sb_attn(q_b, 0, k_p.reshape(bp, lp, nh * hd), v_p.reshape(bp, lp, nh * hd),
                    batch=bp, q_len=lp, tq=256, bk=bk, q_start=0)
    lk_s = past + ls
    lk_pad = -(-lk_s // bk) * bk
    k_all = jnp.concatenate([cache_k_sb[0].reshape(bs, past, nh * hd), k_s.reshape(bs, ls, nh * hd)], axis=1)
    v_all = jnp.concatenate([cache_v_sb[0].reshape(bs, past, nh * hd), v_s.reshape(bs, ls, nh * hd)], axis=1)
    k_all = jnp.pad(k_all, ((0, 0), (0, lk_pad - lk_s), (0, 0)))
    v_all = jnp.pad(v_all, ((0, 0), (0, lk_pad - lk_s), (0, 0)))
    ob_s = _sb_attn(q_b, tp // ls, k_all, v_all, batch=bs, q_len=ls, tq=ls, bk=bk, q_start=past)

    w_router = jnp.concatenate(
        [w_group[0], jnp.moveaxis(w_expert_router[0], 0, 1).reshape(d, N_EXPERTS)], axis=1)
    w_router = jnp.pad(w_router, ((0, 0), (0, LANES - N_GROUPS - N_EXPERTS)))
    wr_hi = w_router.astype(BF16)
    wr_lo = (w_router - wr_hi.astype(F32)).astype(BF16)
    b_router = _pad_lanes(jnp.concatenate([b_group[0], b_expert_router[0].reshape(N_EXPERTS)]))
    h_all, hn_all, route = _merge(oa_p, oa_s, ob_p, ob_s, gates, x_p, x_s,
                                  w_branch_a[0].astype(BF16), w_branch_b[0].astype(BF16),
                                  w_out[0].astype(BF16), norm2_g, wr_hi, wr_lo, b_router)

    t_all = tp + ts
    n_tiles = (2 * t_all + N_EXPERTS * (MOE_TM - 1)) // MOE_TM + 1
    tile_expert, src, n_used, dest = _routing_tables(route, tm=MOE_TM, n_tiles=n_tiles)
    ew1 = w1[0].reshape(N_EXPERTS, d, D_EXPERT)
    ew3 = w3[0].reshape(N_EXPERTS, d, D_EXPERT)
    ew2 = w2[0].reshape(N_EXPERTS, D_EXPERT, d)
    y_sorted = _moe_ffn(tile_expert, src, n_used, hn_all, ew1, ew3, ew2, n_tiles=n_tiles)
    y_p, y_s = _combine(dest, y_sorted, h_all, route, tp=tp, ts=ts)

    conv_p = conv_in[:tp].reshape(bp, lp, a_conv)[:, lp - (CONV_W - 1):]
    conv_s = conv_in[tp:].reshape(bs, ls, a_conv)[:, ls - (CONV_W - 1):]
    return (y_p.reshape(bp, lp, d), y_s.reshape(bs, ls, d),
            conv_p[None], delta_p[None],
            k_p.reshape(1, bp, lp, nh, hd), v_p.reshape(1, bp, lp, nh, hd),
            conv_s[None], delta_s[None],
            k_s.reshape(1, bs, ls, nh, hd), v_s.reshape(1, bs, ls, nh, hd))
```

```python
import functools

import jax
import jax.numpy as jnp
from jax import lax
from jax.experimental import pallas as pl
from jax.experimental.pallas import tpu as pltpu

F32 = jnp.float32
BF16 = jnp.bfloat16

EPS = 1e-6
CHUNK = 64
HEAD_DIM = 128
N_HEADS = 8
D_MODEL = 1024
CONV_W = 4
N_GROUPS = 4
EXPERTS_PER_GROUP = 8
N_EXPERTS = N_GROUPS * EXPERTS_PER_GROUP
D_EXPERT = 512
LANES = 128
SUBLANES = 8
VMEM_LIMIT_BYTES = 56 * 1024 * 1024

PROJ_TM = 512
MERGE_TM = 256
MOE_TM = 256
COMBINE_TM = 256
DELTA_ROWS = 2 * CHUNK


def _dot(a, b):
    return jnp.dot(a, b, preferred_element_type=F32)


def _dot_nt(a, b):
    return lax.dot_general(a, b, (((1,), (1,)), ((), ())), preferred_element_type=F32)


def _split_bf16(x):
    hi = x.astype(BF16)
    lo = (x - hi.astype(F32)).astype(BF16)
    return hi, lo


def _sigmoid(x):
    return 1.0 / (1.0 + jnp.exp(-x))


def _softplus(x):
    return jnp.maximum(x, 0.0) + jnp.log(1.0 + jnp.exp(-jnp.abs(x)))


def _rms_rows(x, g):
    ms = jnp.mean(x * x, axis=-1, keepdims=True)
    return x * lax.rsqrt(ms + EPS) * g


def _proj_kernel(x_ref, g1_ref, w_ref, wba_ref, qg_ref, kg_ref, alog_ref, dtb_ref,
                 conv_ref, z_ref, qb_ref, k_ref, v_ref, gate_ref, ba_ref):
    d = D_MODEL
    xn = _rms_rows(x_ref[...], g1_ref[...]).astype(BF16)

    ba = _dot(xn, wba_ref[...])
    lane = lax.broadcasted_iota(jnp.int32, ba.shape, 1)
    g = -jnp.exp(alog_ref[...]) * _softplus(ba + dtb_ref[...])
    ba_ref[...] = jnp.where(lane < N_HEADS, _sigmoid(ba), g)

    def seg(c):
        return _dot(xn, w_ref[:, c * d:(c + 1) * d])

    def head_rms_store(out_ref, val, g):
        for h in range(N_HEADS):
            sl = slice(h * HEAD_DIM, (h + 1) * HEAD_DIM)
            out_ref[:, sl] = _rms_rows(val[:, sl], g).astype(out_ref.dtype)

    for c in range(3):
        conv_ref[:, c * d:(c + 1) * d] = seg(c).astype(conv_ref.dtype)
    z_ref[...] = seg(3).astype(z_ref.dtype)
    head_rms_store(qb_ref, seg(4), qg_ref[...])
    head_rms_store(k_ref, seg(5), kg_ref[...])
    v_ref[...] = seg(6)
    for c in range(2):
        gate_ref[:, c * d:(c + 1) * d] = _sigmoid(seg(7 + c)).astype(gate_ref.dtype)


def _proj(x, g1, w_main, w_ba, qg, kg, alog_pad, dtb_pad):
    t, d = x.shape
    tm = PROJ_TM
    const = lambda i: (0, 0)
    rows = lambda i: (i, 0)

    def resident(shape):
        return pl.BlockSpec(shape, const, pipeline_mode=pl.Buffered(1))

    in_specs = [
        pl.BlockSpec((tm, d), rows),
        resident((1, d)),
        resident(w_main.shape),
        resident((d, LANES)),
        resident((1, HEAD_DIM)),
        resident((1, HEAD_DIM)),
        resident((1, LANES)),
        resident((1, LANES)),
    ]
    widths_dtypes = [(3 * d, BF16),
                     (d, BF16),
                     (d, BF16),
                     (d, F32),
                     (d, F32),
                     (2 * d, BF16),
                     (LANES, F32)]
    return pl.pallas_call(
        _proj_kernel,
        grid=(t // tm,),
        in_specs=in_specs,
        out_specs=[pl.BlockSpec((tm, w), rows) for w, _ in widths_dtypes],
        out_shape=[jax.ShapeDtypeStruct((t, w), dt) for w, dt in widths_dtypes],
        compiler_params=pltpu.CompilerParams(
            dimension_semantics=("arbitrary",),
            vmem_limit_bytes=VMEM_LIMIT_BYTES),
    )(x, g1, w_main, w_ba, qg, kg, alog_pad, dtb_pad)


def _conv_silu(cur, prev8, w):
    acc = cur * w[CONV_W - 1:CONV_W, :]
    rows8 = lax.broadcasted_iota(jnp.int32, prev8.shape, 0)
    for k in range(1, CONV_W):
        sh = pltpu.roll(cur, k, axis=0)
        ph = pltpu.roll(prev8, k, axis=0)
        head = jnp.where(rows8 < k, ph, sh[0:SUBLANES])
        shifted = jnp.concatenate([head, sh[SUBLANES:]], axis=0)
        acc = acc + shifted * w[CONV_W - 1 - k:CONV_W - k, :]
    return acc * _sigmoid(acc)


def _l2_rows(x):
    return x * lax.rsqrt(jnp.sum(x * x, axis=-1, keepdims=True) + EPS)


def _delta_kernel(conv_ref, hist_ref, cw_ref, ba_ref, z_ref, s0_ref, ng_ref, o_ref, sout_ref,
                  s_ref, tail_ref, *, valid_chunks):
    tt = pl.program_id(1)
    R = DELTA_ROWS
    nh, hd = N_HEADS, HEAD_DIM
    heads = range(nh)

    @pl.when(tt == 0)
    def _():
        s_ref[...] = s0_ref[0]
        tail_ref[...] = hist_ref[0]

    raw = conv_ref[...].astype(F32)
    act = _conv_silu(raw, tail_ref[...], cw_ref[...])
    tail_ref[...] = raw[R - SUBLANES:, :]
    Q = [_l2_rows(act[:, h * hd:(h + 1) * hd]) * (hd ** -0.5) for h in heads]
    K = [_l2_rows(act[:, (nh + h) * hd:(nh + h + 1) * hd]) for h in heads]
    V = [act[:, (2 * nh + h) * hd:(2 * nh + h + 1) * hd] for h in heads]

    row = lax.broadcasted_iota(jnp.int32, (R, R), 0)
    col = lax.broadcasted_iota(jnp.int32, (R, R), 1)
    log2 = lambda n: n.bit_length() - 1
    same_chunk = (row >> log2(CHUNK)) == (col >> log2(CHUNK))
    incl = jnp.logical_and(same_chunk, col <= row)
    strict = jnp.logical_and(same_chunk, col < row)
    cs_mat = jnp.where(incl, 1.0, 0.0).astype(BF16)
    eye = jnp.where(row == col, 1.0, 0.0)

    def level_mask(s):
        same = (row >> log2(2 * s)) == (col >> log2(2 * s))
        return jnp.logical_and(jnp.logical_and(same, ((row >> log2(s)) & 1) == 1), ((col >> log2(s)) & 1) == 0)

    ba = ba_ref[...]
    ba_hi, ba_lo = _split_bf16(ba)
    G_all = _dot(cs_mat, ba_hi) + _dot(cs_mat, ba_lo)
    G_all_t = G_all.T

    beta_b = [jnp.broadcast_to(ba[:, h:h + 1], (R, R)) for h in heads]
    Gb = [jnp.broadcast_to(G_all[:, nh + h:nh + h + 1], (R, R)) for h in heads]
    decay = []
    for h in heads:
        g_row = jnp.broadcast_to(G_all_t[nh + h:nh + h + 1, :], (R, R))
        decay.append(jnp.where(incl, jnp.exp(jnp.where(incl, Gb[h] - g_row, 0.0)), 0.0))
    Kb = [K[h].astype(BF16) for h in heads]
    kq = [_dot_nt(jnp.concatenate([Kb[h], Q[h].astype(BF16)], axis=0), Kb[h]) for h in heads]
    Lm = [jnp.where(strict, beta_b[h] * kq[h][:R] * decay[h], 0.0) for h in heads]
    a_qk = [(kq[h][R:] * decay[h]).astype(BF16) for h in heads]

    T = [eye - jnp.where(level_mask(1), Lm[h], 0.0) for h in heads]
    s = 2
    while s < CHUNK:
        mask = level_mask(s)
        Tb = [T[h].astype(BF16) for h in heads]
        X = [_dot(jnp.where(mask, Lm[h], 0.0).astype(BF16), Tb[h]).astype(BF16) for h in heads]
        T = [T[h] - _dot(Tb[h], X[h]) for h in heads]
        s *= 2

    expG = [jnp.exp(Gb[h]) for h in heads]
    uw = [_dot(T[h].astype(BF16),
               jnp.concatenate([beta_b[h] * V[h], beta_b[h] * expG[h] * K[h]], axis=1).astype(BF16))
          for h in heads]
    g_last = [jnp.concatenate([jnp.broadcast_to(Gb[h][CHUNK - 1:CHUNK, :], (CHUNK, R)),
                               jnp.broadcast_to(Gb[h][R - 1:R, :], (CHUNK, R))], axis=0) for h in heads]
    k_tail_t = [(K[h] * jnp.exp(g_last[h] - Gb[h])).T.astype(BF16) for h in heads]

    zeros_half = jnp.zeros((CHUNK, hd), F32)
    S = [s_ref[h] for h in heads]
    o_rows = []
    for c in range(2):
        if c >= valid_chunks:
            o_rows.append(jnp.zeros((CHUNK, nh * hd), F32))
            continue
        rc = slice(c * CHUNK, (c + 1) * CHUNK)
        lhs_s = [jnp.concatenate([uw[h][rc, hd:], Q[h][rc]], axis=0).astype(BF16) for h in heads]
        lhs_u = [jnp.concatenate([a_qk[h][rc], k_tail_t[h]], axis=0) for h in heads]
        ws = [_dot(lhs_s[h], S[h].astype(BF16)) for h in heads]
        U = [uw[h][rc, :hd] - ws[h][:CHUNK] for h in heads]
        Ucat = [(jnp.concatenate([U[h], zeros_half], axis=0) if c == 0
                 else jnp.concatenate([zeros_half, U[h]], axis=0)).astype(BF16) for h in heads]
        au = [_dot(lhs_u[h], Ucat[h]) for h in heads]
        o_rows.append(jnp.concatenate(
            [expG[h][rc] * ws[h][CHUNK:] + au[h][:CHUNK] for h in heads], axis=1))
        S = [jnp.exp(g_last[h][rc][0:1, :]) * S[h] + au[h][CHUNK:] for h in heads]
    for h in heads:
        s_ref[h] = S[h]
    o_raw = jnp.concatenate(o_rows, axis=0)
    z = z_ref[...].astype(F32)
    gate = z * _sigmoid(z)
    for h in heads:
        sl = slice(h * hd, (h + 1) * hd)
        o_ref[:, sl] = _rms_rows(o_raw[:, sl], ng_ref[...]) * gate[:, sl]

    @pl.when(tt == pl.num_programs(1) - 1)
    def _():
        for h in heads:
            sout_ref[0, h] = S[h]


def _delta(conv_in, row_block_off, hist8, conv_w, ba, z, s0, norm_g, *, batch, length, valid_chunks):
    lt = DELTA_ROWS
    nt = length // lt
    nh, hd = N_HEADS, HEAD_DIM
    c_conv = conv_in.shape[1]

    def rows(b, t):
        return (row_block_off + b * nt + t, 0)

    in_specs = [pl.BlockSpec((lt, c_conv), rows),
                pl.BlockSpec((1, SUBLANES, c_conv), lambda b, t: (b, 0, 0)),
                pl.BlockSpec((CONV_W, c_conv), lambda b, t: (0, 0)),
                pl.BlockSpec((lt, LANES), rows),
                pl.BlockSpec((lt, nh * hd), rows),
                pl.BlockSpec((1, nh, hd, hd), lambda b, t: (b, 0, 0, 0)),
                pl.BlockSpec((1, hd), lambda b, t: (0, 0))]
    out_specs = [pl.BlockSpec((lt, nh * hd), lambda b, t: (b * nt + t, 0)),
                 pl.BlockSpec((1, nh, hd, hd), lambda b, t: (b, 0, 0, 0))]
    out_shape = [jax.ShapeDtypeStruct((batch * length, nh * hd), F32),
                 jax.ShapeDtypeStruct((batch, nh, hd, hd), F32)]
    return pl.pallas_call(
        functools.partial(_delta_kernel, valid_chunks=valid_chunks),
        grid=(batch, nt),
        in_specs=in_specs,
        out_specs=out_specs,
        out_shape=out_shape,
        scratch_shapes=[pltpu.VMEM((nh, hd, hd), F32),
                        pltpu.VMEM((SUBLANES, c_conv), F32)],
        compiler_params=pltpu.CompilerParams(
            dimension_semantics=("arbitrary", "arbitrary"),
            vmem_limit_bytes=VMEM_LIMIT_BYTES),
    )(conv_in, hist8, conv_w, ba, z, s0, norm_g)


def _sb_kernel(q_ref, k_ref, v_ref, o_ref, zz_ref, logb_ref, cs_ref, w_ref, *, tq, bk, q_start):
    i = pl.program_id(2)
    q = (q_ref[...].astype(F32) * (HEAD_DIM ** -0.5)).astype(BF16)
    q_pos0 = q_start + i * tq
    n_kb = (q_pos0 + tq - 1 + bk - 1) // bk
    rr = lax.broadcasted_iota(jnp.int32, (bk, bk), 0)
    cc = lax.broadcasted_iota(jnp.int32, (bk, bk), 1)
    later = jnp.where(rr > cc, 1.0, 0.0).astype(BF16)
    later2 = jnp.concatenate([later, later], axis=0)

    def key_start(j):
        return pl.multiple_of((n_kb - 1 - jnp.clip(j, 0, n_kb - 1)) * bk, bk)

    def scores(j):
        return _dot_nt(q, k_ref[0, pl.ds(key_start(j), bk), :].astype(BF16))

    def stay_terms(zz):
        s = jnp.maximum(zz, 0.0) + jnp.log(1.0 + jnp.exp(-jnp.abs(zz)))
        hi, lo = _split_bf16(s)
        cs = _dot(jnp.concatenate([hi, lo], axis=1), later2)
        return zz - s, cs, jnp.sum(s, axis=1, keepdims=True)

    def weights(acc):
        return jnp.exp(logb_ref[...] - cs_ref[...] - acc).astype(BF16)

    def weighted_values(j, w):
        return _dot(w, v_ref[0, pl.ds(key_start(j), bk), :].astype(BF16))

    def bias(j):
        k_pos = (n_kb - 1 - j) * bk + lax.broadcasted_iota(jnp.int32, (tq, bk), 1)
        q_pos = q_pos0 + lax.broadcasted_iota(jnp.int32, (tq, bk), 0)
        return jnp.where(k_pos < q_pos, 0.0, -1e30)

    logb_ref[...], cs_ref[...], rs = stay_terms(scores(0) + bias(0))
    zz_ref[...] = scores(1) + bias(1) if tq > bk else scores(1)
    w_ref[...] = jnp.zeros_like(w_ref)
    o_ref[...] = jnp.zeros_like(o_ref)

    def body(j, carry):
        acc, rs = carry
        o_ref[...] += weighted_values(j - 1, w_ref[...])
        w_ref[...] = weights(acc)
        zz_next = zz_ref[...]
        zz_ref[...] = scores(j + 2)
        logb_ref[...], cs_ref[...], rs_next = stay_terms(zz_next)
        return acc + rs, rs_next

    acc, _ = lax.fori_loop(0, n_kb - 1, body, (jnp.zeros((tq, 1), F32), rs))
    o_ref[...] += weighted_values(n_kb - 2, w_ref[...]) + weighted_values(n_kb - 1, weights(acc))


def _sb_attn(q, q_row_block_off, k, v, *, batch, q_len, tq, bk, q_start):
    assert tq in (bk, 2 * bk) or (bk % tq == 0 and q_len == tq and q_start % bk == 0)
    nq = q_len // tq
    lk = k.shape[1]
    kv_spec = pl.BlockSpec((1, lk, HEAD_DIM), lambda b, h, i: (b, 0, h))
    return pl.pallas_call(
        functools.partial(_sb_kernel, tq=tq, bk=bk, q_start=q_start),
        grid=(batch, N_HEADS, nq),
        in_specs=[pl.BlockSpec((tq, HEAD_DIM), lambda b, h, i: (q_row_block_off + b * nq + i, h)),
                  kv_spec, kv_spec],
        out_specs=pl.BlockSpec((tq, HEAD_DIM), lambda b, h, i: (b * nq + i, h)),
        out_shape=jax.ShapeDtypeStruct((batch * q_len, N_HEADS * HEAD_DIM), F32),
        scratch_shapes=[pltpu.VMEM((tq, bk), F32)] * 3 + [pltpu.VMEM((tq, bk), BF16)],
        compiler_params=pltpu.CompilerParams(
            dimension_semantics=("arbitrary", "arbitrary", "arbitrary"),
            vmem_limit_bytes=VMEM_LIMIT_BYTES),
    )(q, k, v)


def _merge_kernel(oap_ref, oas_ref, obp_ref, obs_ref, gp_ref, gs_ref, xp_ref, xs_ref,
                  wa_ref, wb_ref, wo_ref, g2_ref, wrh_ref, wrl_ref, br_ref,
                  h_ref, hn_ref, route_ref, *, n_prompt_tiles):
    i = pl.program_id(0)

    def run(oa_ref, ob_ref, gate_ref, x_ref):
        ga = gate_ref[:, :D_MODEL].astype(F32)
        gb = gate_ref[:, D_MODEL:].astype(F32)
        merged = ga * _dot(oa_ref[...].astype(BF16), wa_ref[...]) + gb * _dot(ob_ref[...].astype(BF16), wb_ref[...])
        hh = x_ref[...] + _dot(merged.astype(BF16), wo_ref[...])
        h_ref[...] = hh
        hn = _rms_rows(hh, g2_ref[...])
        hn_ref[...] = hn
        hi, lo = _split_bf16(hn)
        logits = _dot(hi, wrh_ref[...]) + _dot(lo, wrh_ref[...]) + _dot(hi, wrl_ref[...]) + br_ref[...]
        lane_i = lax.broadcasted_iota(jnp.int32, logits.shape, 1)
        lane = lane_i.astype(F32)
        neg = jnp.float32(-jnp.inf)
        big = jnp.float32(LANES)
        is_g = lane_i < N_GROUPS
        lg = jnp.where(is_g, logits, neg)
        g_max = jnp.max(lg, axis=1, keepdims=True)
        g_idx = jnp.min(jnp.where(lg == g_max, lane, big), axis=1, keepdims=True)
        p_top = 1.0 / jnp.sum(jnp.where(is_g, jnp.exp(lg - g_max), 0.0), axis=1, keepdims=True)
        e_lane = lane_i - N_GROUPS
        lane_group = (e_lane >> (EXPERTS_PER_GROUP.bit_length() - 1)).astype(F32)
        sel = jnp.logical_and(jnp.logical_and(e_lane >= 0, e_lane < N_EXPERTS), lane_group == g_idx)
        le = jnp.where(sel, logits, neg)
        m1 = jnp.max(le, axis=1, keepdims=True)
        i1 = jnp.min(jnp.where(le == m1, lane, big), axis=1, keepdims=True)
        le2 = jnp.where(lane == i1, neg, le)
        m2 = jnp.max(le2, axis=1, keepdims=True)
        i2 = jnp.min(jnp.where(le2 == m2, lane, big), axis=1, keepdims=True)
        e2 = jnp.exp(m2 - m1)
        w1 = p_top / (1.0 + e2)
        w2 = p_top * e2 / (1.0 + e2)
        route = jnp.where(lane_i == 0, i1 - N_GROUPS,
                          jnp.where(lane_i == 1, i2 - N_GROUPS,
                                    jnp.where(lane_i == 2, w1, jnp.where(lane_i == 3, w2, 0.0))))
        route_ref[...] = route

    pl.when(i < n_prompt_tiles)(lambda: run(oap_ref, obp_ref, gp_ref, xp_ref))
    pl.when(i >= n_prompt_tiles)(lambda: run(oas_ref, obs_ref, gs_ref, xs_ref))


def _merge(oa_p, oa_s, ob_p, ob_s, gates_p, gates_s, x_p, x_s, wa, wb, wo, g2, wr_hi, wr_lo, br):
    tp, d = x_p.shape
    ts = x_s.shape[0]
    tm = MERGE_TM
    npt = tp // tm
    n_tiles = npt + ts // tm
    t_all = tp + ts
    p_spec = pl.BlockSpec((tm, d), lambda i: (jnp.minimum(i, npt - 1), 0))
    s_spec = pl.BlockSpec((tm, d), lambda i: (jnp.maximum(i - npt, 0), 0))
    const = lambda i: (0, 0)
    w_spec = pl.BlockSpec((d, d), const)
    in_specs = [p_spec, s_spec, p_spec, s_spec,
                pl.BlockSpec((tm, 2 * d), lambda i: (jnp.minimum(i, npt - 1), 0)),
                pl.BlockSpec((tm, 2 * d), lambda i: (jnp.maximum(i - npt, 0), 0)),
                p_spec, s_spec, w_spec, w_spec, w_spec,
                pl.BlockSpec((1, d), const),
                pl.BlockSpec((d, LANES), const), pl.BlockSpec((d, LANES), const),
                pl.BlockSpec((1, LANES), const)]
    out_specs = [pl.BlockSpec((tm, d), lambda i: (i, 0)),
                 pl.BlockSpec((tm, d), lambda i: (i, 0)),
                 pl.BlockSpec((tm, LANES), lambda i: (i, 0))]
    out_shape = [jax.ShapeDtypeStruct((t_all, d), F32),
                 jax.ShapeDtypeStruct((t_all, d), F32),
                 jax.ShapeDtypeStruct((t_all, LANES), F32)]
    return pl.pallas_call(
        functools.partial(_merge_kernel, n_prompt_tiles=npt),
        grid=(n_tiles,),
        in_specs=in_specs,
        out_specs=out_specs,
        out_shape=out_shape,
        compiler_params=pltpu.CompilerParams(
            dimension_semantics=("arbitrary",),
            vmem_limit_bytes=VMEM_LIMIT_BYTES),
    )(oa_p, oa_s, ob_p, ob_s, gates_p, gates_s, x_p, x_s, wa, wb, wo, g2, wr_hi, wr_lo, br)


def _moe_kernel(te_ref, src_ref, nused_ref, hn_hbm, w1_ref, w3_ref, w2_ref, y_ref,
                xbuf, w1b, w3b, w2b, sem, *, tm):
    i = pl.program_id(0)
    n_used = nused_ref[0]
    slot = i % 2

    def row_copy(tile, sl, r):
        tok = src_ref[tile * tm + r]
        return pltpu.make_async_copy(hn_hbm.at[pl.ds(tok, 1), :], xbuf.at[sl, pl.ds(r, 1), :], sem.at[sl])

    def start_gather(tile, sl):
        def issue(r, c):
            row_copy(tile, sl, r).start()
            return c
        lax.fori_loop(0, tm, issue, 0)

    def wait_gather(sl):
        pltpu.make_async_copy(hn_hbm.at[pl.ds(0, tm), :], xbuf.at[sl], sem.at[sl]).wait()

    @pl.when(jnp.logical_and(i == 0, n_used > 0))
    def _():
        start_gather(0, 0)

    @pl.when(i + 1 < n_used)
    def _():
        start_gather(i + 1, 1 - slot)

    @pl.when(i < n_used)
    def _():
        prev = te_ref[jnp.maximum(i - 1, 0)]
        fresh = jnp.logical_or(i == 0, te_ref[i] != prev)

        @pl.when(fresh)
        def _():
            w1b[...] = w1_ref[0].astype(BF16)
            w3b[...] = w3_ref[0].astype(BF16)
            w2b[...] = w2_ref[0].astype(BF16)

        wait_gather(slot)
        x = xbuf[slot].astype(BF16)
        a = _dot(x, w1b[...])
        hid = (a * _sigmoid(a)) * _dot(x, w3b[...])
        y_ref[...] = _dot(hid.astype(BF16), w2b[...])

    @pl.when(i >= n_used)
    def _():
        y_ref[...] = jnp.zeros_like(y_ref)


def _moe_ffn(tile_expert, src, n_used, hn, w1, w3, w2, *, n_tiles):
    tm = MOE_TM
    d = hn.shape[1]
    f = w1.shape[2]
    grid_spec = pltpu.PrefetchScalarGridSpec(
        num_scalar_prefetch=3,
        grid=(n_tiles,),
        in_specs=[pl.BlockSpec(memory_space=pl.ANY),
                  pl.BlockSpec((1, d, f), lambda i, te, src, nu: (te[i], 0, 0)),
                  pl.BlockSpec((1, d, f), lambda i, te, src, nu: (te[i], 0, 0)),
                  pl.BlockSpec((1, f, d), lambda i, te, src, nu: (te[i], 0, 0))],
        out_specs=pl.BlockSpec((tm, d), lambda i, te, src, nu: (i, 0)),
        scratch_shapes=[pltpu.VMEM((2, tm, d), F32),
                        pltpu.VMEM((d, f), BF16), pltpu.VMEM((d, f), BF16), pltpu.VMEM((f, d), BF16),
                        pltpu.SemaphoreType.DMA((2,))])
    return pl.pallas_call(
        functools.partial(_moe_kernel, tm=tm),
        grid_spec=grid_spec,
        out_shape=jax.ShapeDtypeStruct((n_tiles * tm, d), F32),
        compiler_params=pltpu.CompilerParams(
            dimension_semantics=("arbitrary",),
            vmem_limit_bytes=VMEM_LIMIT_BYTES),
    )(tile_expert, src, n_used, hn, w1, w3, w2)


def _combine_kernel(dest_ref, ys_hbm, h_ref, route_ref, yp_ref, ys_ref, buf, sem, *, tm, t_all, n_prompt_tiles):
    i = pl.program_id(0)
    n = pl.num_programs(0)
    slot = i % 2

    def start_gather(tile, sl):
        def issue(r, c):
            for k in range(2):
                row = dest_ref[k * t_all + tile * tm + r]
                pltpu.make_async_copy(ys_hbm.at[pl.ds(row, 1), :], buf.at[sl, k, pl.ds(r, 1), :],
                                      sem.at[sl]).start()
            return c
        lax.fori_loop(0, tm, issue, 0)

    def wait_gather(sl):
        for k in range(2):
            pltpu.make_async_copy(ys_hbm.at[pl.ds(0, tm), :], buf.at[sl, k], sem.at[sl]).wait()

    @pl.when(i == 0)
    def _():
        start_gather(0, 0)

    @pl.when(i + 1 < n)
    def _():
        start_gather(i + 1, 1 - slot)

    wait_gather(slot)
    route = route_ref[...]
    y = h_ref[...] + route[:, 2:3] * buf[slot, 0] + route[:, 3:4] * buf[slot, 1]

    def st(ref):
        ref[...] = y
    pl.when(i < n_prompt_tiles)(lambda: st(yp_ref))
    pl.when(i >= n_prompt_tiles)(lambda: st(ys_ref))


def _combine(dest, y_sorted, h, route, *, tp, ts):
    tm = COMBINE_TM
    d = h.shape[1]
    t_all = tp + ts
    npt = tp // tm
    n_tiles = t_all // tm
    grid_spec = pltpu.PrefetchScalarGridSpec(
        num_scalar_prefetch=1,
        grid=(n_tiles,),
        in_specs=[pl.BlockSpec(memory_space=pl.ANY),
                  pl.BlockSpec((tm, d), lambda i, de: (i, 0)),
                  pl.BlockSpec((tm, LANES), lambda i, de: (i, 0))],
        out_specs=[pl.BlockSpec((tm, d), lambda i, de: (jnp.minimum(i, npt - 1), 0)),
                   pl.BlockSpec((tm, d), lambda i, de: (jnp.maximum(i - npt, 0), 0))],
        scratch_shapes=[pltpu.VMEM((2, 2, tm, d), F32), pltpu.SemaphoreType.DMA((2,))])
    return pl.pallas_call(
        functools.partial(_combine_kernel, tm=tm, t_all=t_all, n_prompt_tiles=npt),
        grid_spec=grid_spec,
        out_shape=[jax.ShapeDtypeStruct((tp, d), F32), jax.ShapeDtypeStruct((ts, d), F32)],
        compiler_params=pltpu.CompilerParams(
            dimension_semantics=("arbitrary",),
            vmem_limit_bytes=VMEM_LIMIT_BYTES),
    )(dest, y_sorted, h, route)


def _routing_tables(route, *, tm, n_tiles):
    t_all = route.shape[0]
    e_flat = jnp.concatenate([route[:, 0], route[:, 1]]).astype(jnp.int32)
    onehot = (e_flat[:, None] == jnp.arange(N_EXPERTS, dtype=jnp.int32)[None, :]).astype(jnp.int32)
    csum = jnp.cumsum(onehot, axis=0)
    rank = jnp.sum(csum * onehot, axis=1) - 1
    counts = csum[-1]
    tiles_e = (counts + tm - 1) // tm
    tile_end = jnp.cumsum(tiles_e)
    tile_start = tile_end - tiles_e
    dest = (tile_start * tm)[e_flat] + rank
    token = jnp.arange(2 * t_all, dtype=jnp.int32) % t_all
    src = jnp.zeros((n_tiles * tm,), jnp.int32).at[dest].set(token, unique_indices=True)
    tile_expert = jnp.minimum(
        jnp.sum((jnp.arange(n_tiles, dtype=jnp.int32)[:, None] >= tile_end[None, :]).astype(jnp.int32), axis=1),
        N_EXPERTS - 1).astype(jnp.int32)
    n_used = tile_end[-1:].astype(jnp.int32)
    return tile_expert, src, n_used, dest.astype(jnp.int32)


def _pad_lanes(v, n=LANES, offset=0):
    out = jnp.zeros((1, n), F32)
    return out.at[0, offset:offset + v.shape[0]].set(v.astype(F32))


def kernel(x_prompt, x_sample, cache_conv_a, state_delta_a, cache_k_sb, cache_v_sb, norm1_g, w_in, conv_a_w, a_log, dt_bias, a_out_norm_g, sb_q_norm_g, sb_k_norm_g, w_branch_a, w_branch_b, w_out, norm2_g, w_group, b_group, w_expert_router, b_expert_router, w1, w3, w2):
    assert norm1_g.shape[0] == 1, "single-layer model"
    bp, lp, d = x_prompt.shape
    bs, ls, _ = x_sample.shape
    past = cache_k_sb.shape[2]
    tp, ts = bp * lp, bs * ls
    nh, hd = N_HEADS, HEAD_DIM
    a_conv = 3 * nh * hd

    wi = w_in[0]
    c_z = a_conv + nh * hd
    c_ba = c_z + 2 * nh
    w_main = jnp.concatenate([wi[:, :c_z], wi[:, c_ba:]], axis=1).astype(BF16)
    w_ba = jnp.pad(wi[:, c_z:c_ba], ((0, 0), (0, LANES - 2 * nh))).astype(BF16)
    alog_pad = _pad_lanes(a_log[0], offset=nh)
    dtb_pad = _pad_lanes(dt_bias[0], offset=nh)
    x_p = x_prompt.reshape(tp, d)
    x_s = x_sample.reshape(ts, d)

    proj_w = (norm1_g, w_main, w_ba, sb_q_norm_g, sb_k_norm_g, alog_pad, dtb_pad)
    conv_in_p, z_p, q_p, k_p, v_p, gates_p, ba_p = _proj(x_p, *proj_w)
    conv_in_s, z_s, q_s, k_s, v_s, gates_s, ba_s = _proj(x_s, *proj_w)

    conv_w = conv_a_w[0]
    hist_p = jnp.zeros((bp, SUBLANES, a_conv), F32)
    s0_p = jnp.zeros((bp, nh, hd, hd), F32)
    oa_p, delta_p = _delta(conv_in_p, 0, hist_p, conv_w, ba_p, z_p, s0_p, a_out_norm_g,
                           batch=bp, length=lp, valid_chunks=DELTA_ROWS // CHUNK)
    pad_rows = DELTA_ROWS - ls

    def pad_stream(a):
        a = a.reshape(bs, ls, a.shape[-1])
        return jnp.pad(a, ((0, 0), (0, pad_rows), (0, 0))).reshape(bs * DELTA_ROWS, a.shape[-1])

    hist_s = jnp.pad(cache_conv_a[0], ((0, 0), (SUBLANES - (CONV_W - 1), 0), (0, 0)))
    oa_s_pad, delta_s = _delta(pad_stream(conv_in_s), 0, hist_s, conv_w, pad_stream(ba_s), pad_stream(z_s),
                               state_delta_a[0], a_out_norm_g,
                               batch=bs, length=DELTA_ROWS, valid_chunks=ls // CHUNK)
    oa_s = oa_s_pad.reshape(bs, DELTA_ROWS, nh * hd)[:, :ls].reshape(ts, nh * hd)

    bk = 256
    ob_p = _sb_attn(q_p, 0, k_p.reshape(bp, lp, nh * hd), v_p.reshape(bp, lp, nh * hd),
                    batch=bp, q_len=lp, tq=2 * bk, bk=bk, q_start=0)
    lk_s = past + ls
    lk_pad = -(-lk_s // bk) * bk
    k_all = jnp.concatenate([cache_k_sb[0].reshape(bs, past, nh * hd), k_s.reshape(bs, ls, nh * hd)], axis=1)
    v_all = jnp.concatenate([cache_v_sb[0].reshape(bs, past, nh * hd), v_s.reshape(bs, ls, nh * hd)], axis=1)
    k_all = jnp.pad(k_all, ((0, 0), (0, lk_pad - lk_s), (0, 0)))
    v_all = jnp.pad(v_all, ((0, 0), (0, lk_pad - lk_s), (0, 0)))
    ob_s = _sb_attn(q_s, 0, k_all, v_all, batch=bs, q_len=ls, tq=ls, bk=bk, q_start=past)

    w_router = jnp.concatenate(
        [w_group[0], jnp.moveaxis(w_expert_router[0], 0, 1).reshape(d, N_EXPERTS)], axis=1)
    w_router = jnp.pad(w_router, ((0, 0), (0, LANES - N_GROUPS - N_EXPERTS)))
    wr_hi = w_router.astype(BF16)
    wr_lo = (w_router - wr_hi.astype(F32)).astype(BF16)
    b_router = _pad_lanes(jnp.concatenate([b_group[0], b_expert_router[0].reshape(N_EXPERTS)]))
    h_all, hn_all, route = _merge(oa_p, oa_s, ob_p, ob_s, gates_p, gates_s, x_p, x_s,
                                  w_branch_a[0].astype(BF16), w_branch_b[0].astype(BF16),
                                  w_out[0].astype(BF16), norm2_g, wr_hi, wr_lo, b_router)

    t_all = tp + ts
    n_tiles = (2 * t_all + N_EXPERTS * (MOE_TM - 1)) // MOE_TM + 1
    tile_expert, src, n_used, dest = _routing_tables(route, tm=MOE_TM, n_tiles=n_tiles)
    ew1 = w1[0].reshape(N_EXPERTS, d, D_EXPERT)
    ew3 = w3[0].reshape(N_EXPERTS, d, D_EXPERT)
    ew2 = w2[0].reshape(N_EXPERTS, D_EXPERT, d)
    y_sorted = _moe_ffn(tile_expert, src, n_used, hn_all, ew1, ew3, ew2, n_tiles=n_tiles)
    y_p, y_s = _combine(dest, y_sorted, h_all, route, tp=tp, ts=ts)

    def last_rows(conv_in, n, length):
        return jnp.stack([conv_in[(b + 1) * length - (CONV_W - 1):(b + 1) * length]
                          for b in range(n)]).astype(F32)
    conv_p = last_rows(conv_in_p, bp, lp)
    conv_s = last_rows(conv_in_s, bs, ls)
    return (y_p.reshape(bp, lp, d), y_s.reshape(bs, ls, d),
            conv_p[None], delta_p[None],
            k_p.reshape(1, bp, lp, nh, hd), v_p.reshape(1, bp, lp, nh, hd),
            conv_s[None], delta_s[None],
            k_s.reshape(1, bs, ls, nh, hd), v_s.reshape(1, bs, ls, nh, hd))
```

```python
import functools

import jax
import jax.numpy as jnp
from jax import lax
from jax.experimental import pallas as pl
from jax.experimental.pallas import tpu as pltpu

F32 = jnp.float32
BF16 = jnp.bfloat16

EPS = 1e-6
CHUNK = 64
HEAD_DIM = 128
N_HEADS = 8
D_MODEL = 1024
CONV_W = 4
N_GROUPS = 4
EXPERTS_PER_GROUP = 8
N_EXPERTS = N_GROUPS * EXPERTS_PER_GROUP
D_EXPERT = 512
LANES = 128
SUBLANES = 8
VMEM_LIMIT_BYTES = 56 * 1024 * 1024

PROJ_TM = 512
MERGE_TM = 256
MOE_TM = 256
COMBINE_TM = 256
DELTA_ROWS = 2 * CHUNK
_SB_DEAD = 110.0


def _dot(a, b):
    return jnp.dot(a, b, preferred_element_type=F32)


def _dot_nt(a, b):
    return lax.dot_general(a, b, (((1,), (1,)), ((), ())), preferred_element_type=F32)


def _split_bf16(x):
    hi = x.astype(BF16)
    lo = (x - hi.astype(F32)).astype(BF16)
    return hi, lo


def _sigmoid(x):
    return 1.0 / (1.0 + jnp.exp(-x))


def _softplus(x):
    return jnp.maximum(x, 0.0) + jnp.log(1.0 + jnp.exp(-jnp.abs(x)))


def _rms_rows(x, g):
    ms = jnp.mean(x * x, axis=-1, keepdims=True)
    return x * lax.rsqrt(ms + EPS) * g


assert D_MODEL == SUBLANES * LANES


def _token_rows(t):
    return pl.ds(pl.multiple_of(t * SUBLANES, SUBLANES), SUBLANES)


def _to_token_tiles(ref, val):
    for s in range(SUBLANES):
        ref[pl.ds(s, val.shape[0], stride=SUBLANES), :] = val[:, s * LANES:(s + 1) * LANES]


def _token_tile_chunk(ref, n, s):
    return ref[pl.ds(s, n, stride=SUBLANES), :]


def _from_token_tiles(ref, n):
    return jnp.concatenate([_token_tile_chunk(ref, n, s) for s in range(SUBLANES)], axis=1)


def _proj_kernel(x_ref, g1_ref, w_ref, wba_ref, qg_ref, kg_ref, alog_ref, dtb_ref,
                 conv_ref, z_ref, qb_ref, k_ref, v_ref, gate_ref, ba_ref):
    d = D_MODEL
    xn = _rms_rows(x_ref[...], g1_ref[...]).astype(BF16)

    ba = _dot(xn, wba_ref[...])
    lane = lax.broadcasted_iota(jnp.int32, ba.shape, 1)
    g = -jnp.exp(alog_ref[...]) * _softplus(ba + dtb_ref[...])
    ba_ref[...] = jnp.where(lane < N_HEADS, _sigmoid(ba), g)

    def seg(c):
        return _dot(xn, w_ref[:, c * d:(c + 1) * d])

    def head_rms_store(out_ref, val, g):
        for h in range(N_HEADS):
            sl = slice(h * HEAD_DIM, (h + 1) * HEAD_DIM)
            out_ref[:, sl] = _rms_rows(val[:, sl], g).astype(out_ref.dtype)

    for c in range(3):
        conv_ref[:, c * d:(c + 1) * d] = seg(c).astype(conv_ref.dtype)
    z_ref[...] = seg(3).astype(z_ref.dtype)
    head_rms_store(qb_ref, seg(4), qg_ref[...])
    head_rms_store(k_ref, seg(5), kg_ref[...])
    v_ref[...] = seg(6)
    for c in range(2):
        gate_ref[:, c * d:(c + 1) * d] = _sigmoid(seg(7 + c)).astype(gate_ref.dtype)


def _proj(x, g1, w_main, w_ba, qg, kg, alog_pad, dtb_pad):
    t, d = x.shape
    tm = PROJ_TM
    const = lambda i: (0, 0)
    rows = lambda i: (i, 0)

    def resident(shape):
        return pl.BlockSpec(shape, const, pipeline_mode=pl.Buffered(1))

    in_specs = [
        pl.BlockSpec((tm, d), rows),
        resident((1, d)),
        resident(w_main.shape),
        resident((d, LANES)),
        resident((1, HEAD_DIM)),
        resident((1, HEAD_DIM)),
        resident((1, LANES)),
        resident((1, LANES)),
    ]
    widths_dtypes = [(3 * d, BF16),
                     (d, BF16),
                     (d, BF16),
                     (d, F32),
                     (d, F32),
                     (2 * d, BF16),
                     (LANES, F32)]
    return pl.pallas_call(
        _proj_kernel,
        grid=(t // tm,),
        in_specs=in_specs,
        out_specs=[pl.BlockSpec((tm, w), rows) for w, _ in widths_dtypes],
        out_shape=[jax.ShapeDtypeStruct((t, w), dt) for w, dt in widths_dtypes],
        compiler_params=pltpu.CompilerParams(
            dimension_semantics=("arbitrary",),
            vmem_limit_bytes=VMEM_LIMIT_BYTES),
    )(x, g1, w_main, w_ba, qg, kg, alog_pad, dtb_pad)


def _conv_silu(cur, prev8, w):
    acc = cur * w[CONV_W - 1:CONV_W, :]
    rows8 = lax.broadcasted_iota(jnp.int32, prev8.shape, 0)
    for k in range(1, CONV_W):
        sh = pltpu.roll(cur, k, axis=0)
        ph = pltpu.roll(prev8, k, axis=0)
        head = jnp.where(rows8 < k, ph, sh[0:SUBLANES])
        shifted = jnp.concatenate([head, sh[SUBLANES:]], axis=0)
        acc = acc + shifted * w[CONV_W - 1 - k:CONV_W - k, :]
    return acc * _sigmoid(acc)


def _l2_rows(x):
    return x * lax.rsqrt(jnp.sum(x * x, axis=-1, keepdims=True) + EPS)


def _delta_kernel(conv_ref, hist_ref, cw_ref, ba_ref, z_ref, s0_ref, ng_ref, o_ref, sout_ref,
                  s_ref, tail_ref, *, valid_chunks):
    tt = pl.program_id(1)
    R = DELTA_ROWS
    nh, hd = N_HEADS, HEAD_DIM
    heads = range(nh)

    @pl.when(tt == 0)
    def _():
        s_ref[...] = s0_ref[0]
        tail_ref[...] = hist_ref[0]

    raw = conv_ref[...].astype(F32)
    act = _conv_silu(raw, tail_ref[...], cw_ref[...])
    tail_ref[...] = raw[R - SUBLANES:, :]
    Q = [_l2_rows(act[:, h * hd:(h + 1) * hd]) * (hd ** -0.5) for h in heads]
    K = [_l2_rows(act[:, (nh + h) * hd:(nh + h + 1) * hd]) for h in heads]
    V = [act[:, (2 * nh + h) * hd:(2 * nh + h + 1) * hd] for h in heads]

    row = lax.broadcasted_iota(jnp.int32, (R, R), 0)
    col = lax.broadcasted_iota(jnp.int32, (R, R), 1)
    log2 = lambda n: n.bit_length() - 1
    same_chunk = (row >> log2(CHUNK)) == (col >> log2(CHUNK))
    incl = jnp.logical_and(same_chunk, col <= row)
    strict = jnp.logical_and(same_chunk, col < row)
    cs_mat = jnp.where(incl, 1.0, 0.0).astype(BF16)
    eye = jnp.where(row == col, 1.0, 0.0)

    def level_mask(s):
        same = (row >> log2(2 * s)) == (col >> log2(2 * s))
        return jnp.logical_and(jnp.logical_and(same, ((row >> log2(s)) & 1) == 1), ((col >> log2(s)) & 1) == 0)

    ba = ba_ref[...]
    ba_hi, ba_lo = _split_bf16(ba)
    G_all = _dot(cs_mat, ba_hi) + _dot(cs_mat, ba_lo)
    G_all_t = G_all.T

    beta_b = [jnp.broadcast_to(ba[:, h:h + 1], (R, R)) for h in heads]
    Gb = [jnp.broadcast_to(G_all[:, nh + h:nh + h + 1], (R, R)) for h in heads]
    decay = []
    for h in heads:
        g_row = jnp.broadcast_to(G_all_t[nh + h:nh + h + 1, :], (R, R))
        decay.append(jnp.where(incl, jnp.exp(jnp.where(incl, Gb[h] - g_row, 0.0)), 0.0))
    Kb = [K[h].astype(BF16) for h in heads]
    kq = [_dot_nt(jnp.concatenate([Kb[h], Q[h].astype(BF16)], axis=0), Kb[h]) for h in heads]
    Lm = [jnp.where(strict, beta_b[h] * kq[h][:R] * decay[h], 0.0) for h in heads]
    a_qk = [(kq[h][R:] * decay[h]).astype(BF16) for h in heads]

    T = [eye - jnp.where(level_mask(1), Lm[h], 0.0) for h in heads]
    s = 2
    while s < CHUNK:
        mask = level_mask(s)
        Tb = [T[h].astype(BF16) for h in heads]
        X = [_dot(jnp.where(mask, Lm[h], 0.0).astype(BF16), Tb[h]).astype(BF16) for h in heads]
        T = [T[h] - _dot(Tb[h], X[h]) for h in heads]
        s *= 2

    expG = [jnp.exp(Gb[h]) for h in heads]
    uw = [_dot(T[h].astype(BF16),
               jnp.concatenate([beta_b[h] * V[h], beta_b[h] * expG[h] * K[h]], axis=1).astype(BF16))
          for h in heads]
    g_last = [jnp.concatenate([jnp.broadcast_to(Gb[h][CHUNK - 1:CHUNK, :], (CHUNK, R)),
                               jnp.broadcast_to(Gb[h][R - 1:R, :], (CHUNK, R))], axis=0) for h in heads]
    k_tail_t = [(K[h] * jnp.exp(g_last[h] - Gb[h])).T.astype(BF16) for h in heads]

    zeros_half = jnp.zeros((CHUNK, hd), F32)
    S = [s_ref[h] for h in heads]
    o_rows = []
    for c in range(2):
        if c >= valid_chunks:
            o_rows.append(jnp.zeros((CHUNK, nh * hd), F32))
            continue
        rc = slice(c * CHUNK, (c + 1) * CHUNK)
        lhs_s = [jnp.concatenate([uw[h][rc, hd:], Q[h][rc]], axis=0).astype(BF16) for h in heads]
        lhs_u = [jnp.concatenate([a_qk[h][rc], k_tail_t[h]], axis=0) for h in heads]
        ws = [_dot(lhs_s[h], S[h].astype(BF16)) for h in heads]
        U = [uw[h][rc, :hd] - ws[h][:CHUNK] for h in heads]
        Ucat = [(jnp.concatenate([U[h], zeros_half], axis=0) if c == 0
                 else jnp.concatenate([zeros_half, U[h]], axis=0)).astype(BF16) for h in heads]
        au = [_dot(lhs_u[h], Ucat[h]) for h in heads]
        o_rows.append(jnp.concatenate(
            [expG[h][rc] * ws[h][CHUNK:] + au[h][:CHUNK] for h in heads], axis=1))
        S = [jnp.exp(g_last[h][rc][0:1, :]) * S[h] + au[h][CHUNK:] for h in heads]
    for h in heads:
        s_ref[h] = S[h]
    o_raw = jnp.concatenate(o_rows, axis=0)
    z = z_ref[...].astype(F32)
    gate = z * _sigmoid(z)
    for h in heads:
        sl = slice(h * hd, (h + 1) * hd)
        o_ref[:, sl] = _rms_rows(o_raw[:, sl], ng_ref[...]) * gate[:, sl]

    @pl.when(tt == pl.num_programs(1) - 1)
    def _():
        for h in heads:
            sout_ref[0, h] = S[h]


def _delta(conv_in, row_block_off, hist8, conv_w, ba, z, s0, norm_g, *, batch, length, valid_chunks):
    lt = DELTA_ROWS
    nt = length // lt
    nh, hd = N_HEADS, HEAD_DIM
    c_conv = conv_in.shape[1]

    def rows(b, t):
        return (row_block_off + b * nt + t, 0)

    in_specs = [pl.BlockSpec((lt, c_conv), rows),
                pl.BlockSpec((1, SUBLANES, c_conv), lambda b, t: (b, 0, 0)),
                pl.BlockSpec((CONV_W, c_conv), lambda b, t: (0, 0)),
                pl.BlockSpec((lt, LANES), rows),
                pl.BlockSpec((lt, nh * hd), rows),
                pl.BlockSpec((1, nh, hd, hd), lambda b, t: (b, 0, 0, 0)),
                pl.BlockSpec((1, hd), lambda b, t: (0, 0))]
    out_specs = [pl.BlockSpec((lt, nh * hd), lambda b, t: (b * nt + t, 0)),
                 pl.BlockSpec((1, nh, hd, hd), lambda b, t: (b, 0, 0, 0))]
    out_shape = [jax.ShapeDtypeStruct((batch * length, nh * hd), F32),
                 jax.ShapeDtypeStruct((batch, nh, hd, hd), F32)]
    return pl.pallas_call(
        functools.partial(_delta_kernel, valid_chunks=valid_chunks),
        grid=(batch, nt),
        in_specs=in_specs,
        out_specs=out_specs,
        out_shape=out_shape,
        scratch_shapes=[pltpu.VMEM((nh, hd, hd), F32),
                        pltpu.VMEM((SUBLANES, c_conv), F32)],
        compiler_params=pltpu.CompilerParams(
            dimension_semantics=("arbitrary", "arbitrary"),
            vmem_limit_bytes=VMEM_LIMIT_BYTES),
    )(conv_in, hist8, conv_w, ba, z, s0, norm_g)


def _sb_kernel(q_ref, k_ref, v_ref, o_ref, zz_ref, logb_ref, cs_ref, w_ref, *, tq, bk, q_start):
    i = pl.program_id(2)
    q = (q_ref[...].astype(F32) * (HEAD_DIM ** -0.5)).astype(BF16)
    q_pos0 = q_start + i * tq
    n_kb = (q_pos0 + tq - 1 + bk - 1) // bk
    rr = lax.broadcasted_iota(jnp.int32, (bk, bk), 0)
    cc = lax.broadcasted_iota(jnp.int32, (bk, bk), 1)
    later = jnp.where(rr > cc, 1.0, 0.0).astype(BF16)
    later2 = jnp.concatenate([later, later], axis=0)

    def key_start(j):
        return pl.multiple_of((n_kb - 1 - jnp.clip(j, 0, n_kb - 1)) * bk, bk)

    def scores(j):
        return _dot_nt(q, k_ref[0, pl.ds(key_start(j), bk), :].astype(BF16))

    def stay_terms(zz):
        s = jnp.maximum(zz, 0.0) + jnp.log(1.0 + jnp.exp(-jnp.abs(zz)))
        hi, lo = _split_bf16(s)
        cs = _dot(jnp.concatenate([hi, lo], axis=1), later2)
        return zz - s, cs, jnp.sum(s, axis=1, keepdims=True)

    def weights(acc):
        return jnp.exp(logb_ref[...] - cs_ref[...] - acc).astype(BF16)

    def weighted_values(j, w):
        return _dot(w, v_ref[0, pl.ds(key_start(j), bk), :].astype(BF16))

    def bias(j):
        k_pos = (n_kb - 1 - j) * bk + lax.broadcasted_iota(jnp.int32, (tq, bk), 1)
        q_pos = q_pos0 + lax.broadcasted_iota(jnp.int32, (tq, bk), 0)
        return jnp.where(k_pos < q_pos, 0.0, -1e30)

    logb_ref[...], cs_ref[...], rs = stay_terms(scores(0) + bias(0))
    zz_ref[...] = scores(1) + bias(1) if tq > bk else scores(1)
    w_ref[...] = jnp.zeros_like(w_ref)
    o_ref[...] = jnp.zeros_like(o_ref)

    def live(acc):
        return (jnp.min(acc) < _SB_DEAD).astype(jnp.int32)

    def cond(carry):
        j, go, _, _ = carry
        return jnp.logical_and(j < n_kb - 1, go > 0)

    def body(carry):
        j, _, acc, rs = carry
        acc_next = acc + rs
        go_next = live(acc_next)
        o_ref[...] += weighted_values(j - 1, w_ref[...])
        w_ref[...] = weights(acc)
        zz_next = zz_ref[...]
        zz_ref[...] = scores(j + 2)
        logb_ref[...], cs_ref[...], rs_next = stay_terms(zz_next)
        return j + 1, go_next, acc_next, rs_next

    j, _, acc, _ = lax.while_loop(cond, body, (jnp.int32(0), jnp.int32(1), jnp.zeros((tq, 1), F32), rs))
    o_ref[...] += weighted_values(j - 1, w_ref[...]) + weighted_values(j, weights(acc))


def _sb_attn(q, q_row_block_off, k, v, *, batch, q_len, tq, bk, q_start):
    assert tq in (bk, 2 * bk) or (bk % tq == 0 and q_len == tq and q_start % bk == 0)
    nq = q_len // tq
    lk = k.shape[1]
    kv_spec = pl.BlockSpec((1, lk, HEAD_DIM), lambda b, h, i: (b, 0, h))
    return pl.pallas_call(
        functools.partial(_sb_kernel, tq=tq, bk=bk, q_start=q_start),
        grid=(batch, N_HEADS, nq),
        in_specs=[pl.BlockSpec((tq, HEAD_DIM), lambda b, h, i: (q_row_block_off + b * nq + i, h)),
                  kv_spec, kv_spec],
        out_specs=pl.BlockSpec((tq, HEAD_DIM), lambda b, h, i: (b * nq + i, h)),
        out_shape=jax.ShapeDtypeStruct((batch * q_len, N_HEADS * HEAD_DIM), F32),
        scratch_shapes=[pltpu.VMEM((tq, bk), F32)] * 3 + [pltpu.VMEM((tq, bk), BF16)],
        compiler_params=pltpu.CompilerParams(
            dimension_semantics=("arbitrary", "arbitrary", "arbitrary"),
            vmem_limit_bytes=VMEM_LIMIT_BYTES),
    )(q, k, v)


def _merge_kernel(oap_ref, oas_ref, obp_ref, obs_ref, gp_ref, gs_ref, xp_ref, xs_ref,
                  wa_ref, wb_ref, wo_ref, g2_ref, wrh_ref, wrl_ref, br_ref,
                  h_ref, hn_ref, route_ref, *, n_prompt_tiles):
    i = pl.program_id(0)

    def run(oa_ref, ob_ref, gate_ref, x_ref):
        ga = gate_ref[:, :D_MODEL].astype(F32)
        gb = gate_ref[:, D_MODEL:].astype(F32)
        merged = ga * _dot(oa_ref[...].astype(BF16), wa_ref[...]) + gb * _dot(ob_ref[...].astype(BF16), wb_ref[...])
        hh = x_ref[...] + _dot(merged.astype(BF16), wo_ref[...])
        h_ref[...] = hh
        hn = _rms_rows(hh, g2_ref[...])
        _to_token_tiles(hn_ref, hn)
        hi, lo = _split_bf16(hn)
        logits = _dot(hi, wrh_ref[...]) + _dot(lo, wrh_ref[...]) + _dot(hi, wrl_ref[...]) + br_ref[...]
        lane_i = lax.broadcasted_iota(jnp.int32, logits.shape, 1)
        lane = lane_i.astype(F32)
        neg = jnp.float32(-jnp.inf)
        big = jnp.float32(LANES)
        is_g = lane_i < N_GROUPS
        lg = jnp.where(is_g, logits, neg)
        g_max = jnp.max(lg, axis=1, keepdims=True)
        g_idx = jnp.min(jnp.where(lg == g_max, lane, big), axis=1, keepdims=True)
        p_top = 1.0 / jnp.sum(jnp.where(is_g, jnp.exp(lg - g_max), 0.0), axis=1, keepdims=True)
        e_lane = lane_i - N_GROUPS
        lane_group = (e_lane >> (EXPERTS_PER_GROUP.bit_length() - 1)).astype(F32)
        sel = jnp.logical_and(jnp.logical_and(e_lane >= 0, e_lane < N_EXPERTS), lane_group == g_idx)
        le = jnp.where(sel, logits, neg)
        m1 = jnp.max(le, axis=1, keepdims=True)
        i1 = jnp.min(jnp.where(le == m1, lane, big), axis=1, keepdims=True)
        le2 = jnp.where(lane == i1, neg, le)
        m2 = jnp.max(le2, axis=1, keepdims=True)
        i2 = jnp.min(jnp.where(le2 == m2, lane, big), axis=1, keepdims=True)
        e2 = jnp.exp(m2 - m1)
        w1 = p_top / (1.0 + e2)
        w2 = p_top * e2 / (1.0 + e2)
        route = jnp.where(lane_i == 0, i1 - N_GROUPS,
                          jnp.where(lane_i == 1, i2 - N_GROUPS,
                                    jnp.where(lane_i == 2, w1, jnp.where(lane_i == 3, w2, 0.0))))
        route_ref[...] = route

    pl.when(i < n_prompt_tiles)(lambda: run(oap_ref, obp_ref, gp_ref, xp_ref))
    pl.when(i >= n_prompt_tiles)(lambda: run(oas_ref, obs_ref, gs_ref, xs_ref))


def _merge(oa_p, oa_s, ob_p, ob_s, gates_p, gates_s, x_p, x_s, wa, wb, wo, g2, wr_hi, wr_lo, br):
    tp, d = x_p.shape
    ts = x_s.shape[0]
    tm = MERGE_TM
    npt = tp // tm
    n_tiles = npt + ts // tm
    t_all = tp + ts
    p_spec = pl.BlockSpec((tm, d), lambda i: (jnp.minimum(i, npt - 1), 0))
    s_spec = pl.BlockSpec((tm, d), lambda i: (jnp.maximum(i - npt, 0), 0))
    const = lambda i: (0, 0)
    w_spec = pl.BlockSpec((d, d), const)
    in_specs = [p_spec, s_spec, p_spec, s_spec,
                pl.BlockSpec((tm, 2 * d), lambda i: (jnp.minimum(i, npt - 1), 0)),
                pl.BlockSpec((tm, 2 * d), lambda i: (jnp.maximum(i - npt, 0), 0)),
                p_spec, s_spec, w_spec, w_spec, w_spec,
                pl.BlockSpec((1, d), const),
                pl.BlockSpec((d, LANES), const), pl.BlockSpec((d, LANES), const),
                pl.BlockSpec((1, LANES), const)]
    out_specs = [pl.BlockSpec((tm, d), lambda i: (i, 0)),
                 pl.BlockSpec((tm * SUBLANES, LANES), lambda i: (i, 0)),
                 pl.BlockSpec((tm, LANES), lambda i: (i, 0))]
    out_shape = [jax.ShapeDtypeStruct((t_all, d), F32),
                 jax.ShapeDtypeStruct((t_all * SUBLANES, LANES), F32),
                 jax.ShapeDtypeStruct((t_all, LANES), F32)]
    return pl.pallas_call(
        functools.partial(_merge_kernel, n_prompt_tiles=npt),
        grid=(n_tiles,),
        in_specs=in_specs,
        out_specs=out_specs,
        out_shape=out_shape,
        compiler_params=pltpu.CompilerParams(
            dimension_semantics=("arbitrary",),
            vmem_limit_bytes=VMEM_LIMIT_BYTES),
    )(oa_p, oa_s, ob_p, ob_s, gates_p, gates_s, x_p, x_s, wa, wb, wo, g2, wr_hi, wr_lo, br)


def _moe_kernel(te_ref, src_ref, nused_ref, hn_hbm, w1_ref, w3_ref, w2_ref, y_ref,
                xbuf, w1b, w3b, w2b, sem, *, tm):
    i = pl.program_id(0)
    n_used = nused_ref[0]
    slot = i % 2

    def token_copy(tile, sl, r):
        tok = src_ref[tile * tm + r]
        return pltpu.make_async_copy(hn_hbm.at[_token_rows(tok)], xbuf.at[sl, _token_rows(r)], sem.at[sl])

    def start_gather(tile, sl):
        def issue(r8, c):
            for k in range(SUBLANES):
                token_copy(tile, sl, r8 * SUBLANES + k).start()
            return c
        lax.fori_loop(0, tm // SUBLANES, issue, 0)

    def wait_gather(sl):
        pltpu.make_async_copy(hn_hbm.at[pl.ds(0, tm * SUBLANES)], xbuf.at[sl], sem.at[sl]).wait()

    @pl.when(jnp.logical_and(i == 0, n_used > 0))
    def _():
        start_gather(0, 0)

    @pl.when(i + 1 < n_used)
    def _():
        start_gather(i + 1, 1 - slot)

    @pl.when(i < n_used)
    def _():
        prev = te_ref[jnp.maximum(i - 1, 0)]
        fresh = jnp.logical_or(i == 0, te_ref[i] != prev)

        @pl.when(fresh)
        def _():
            w1b[...] = w1_ref[0].astype(BF16)
            w3b[...] = w3_ref[0].astype(BF16)
            w2b[...] = w2_ref[0].astype(BF16)

        wait_gather(slot)
        x = _from_token_tiles(xbuf.at[slot], tm).astype(BF16)
        a = _dot(x, w1b[...])
        hid = (a * _sigmoid(a)) * _dot(x, w3b[...])
        _to_token_tiles(y_ref, _dot(hid.astype(BF16), w2b[...]))

    @pl.when(i >= n_used)
    def _():
        y_ref[...] = jnp.zeros_like(y_ref)


def _moe_ffn(tile_expert, src, n_used, hn_tiles, w1, w3, w2, *, n_tiles):
    tm = MOE_TM
    d = w1.shape[1]
    f = w1.shape[2]
    grid_spec = pltpu.PrefetchScalarGridSpec(
        num_scalar_prefetch=3,
        grid=(n_tiles,),
        in_specs=[pl.BlockSpec(memory_space=pl.ANY),
                  pl.BlockSpec((1, d, f), lambda i, te, src, nu: (te[i], 0, 0)),
                  pl.BlockSpec((1, d, f), lambda i, te, src, nu: (te[i], 0, 0)),
                  pl.BlockSpec((1, f, d), lambda i, te, src, nu: (te[i], 0, 0))],
        out_specs=pl.BlockSpec((tm * SUBLANES, LANES), lambda i, te, src, nu: (i, 0)),
        scratch_shapes=[pltpu.VMEM((2, tm * SUBLANES, LANES), F32),
                        pltpu.VMEM((d, f), BF16), pltpu.VMEM((d, f), BF16), pltpu.VMEM((f, d), BF16),
                        pltpu.SemaphoreType.DMA((2,))])
    return pl.pallas_call(
        functools.partial(_moe_kernel, tm=tm),
        grid_spec=grid_spec,
        out_shape=jax.ShapeDtypeStruct((n_tiles * tm * SUBLANES, LANES), F32),
        compiler_params=pltpu.CompilerParams(
            dimension_semantics=("arbitrary",),
            vmem_limit_bytes=VMEM_LIMIT_BYTES),
    )(tile_expert, src, n_used, hn_tiles, w1, w3, w2)


def _combine_kernel(dest_ref, ys_hbm, h_ref, route_ref, yp_ref, ys_ref, buf, sem, *, tm, t_all, n_prompt_tiles):
    i = pl.program_id(0)
    n = pl.num_programs(0)
    slot = i % 2

    def start_gather(tile, sl):
        def issue(r8, c):
            for j in range(SUBLANES):
                r = r8 * SUBLANES + j
                for k in range(2):
                    row = dest_ref[k * t_all + tile * tm + r]
                    pltpu.make_async_copy(ys_hbm.at[_token_rows(row)], buf.at[sl, k, _token_rows(r)],
                                          sem.at[sl]).start()
            return c
        lax.fori_loop(0, tm // SUBLANES, issue, 0)

    def wait_gather(sl):
        for k in range(2):
            pltpu.make_async_copy(ys_hbm.at[pl.ds(0, tm * SUBLANES)], buf.at[sl, k], sem.at[sl]).wait()

    @pl.when(i == 0)
    def _():
        start_gather(0, 0)

    @pl.when(i + 1 < n)
    def _():
        start_gather(i + 1, 1 - slot)

    wait_gather(slot)
    route = route_ref[...]
    w0 = jnp.broadcast_to(route[:, 2:3], (tm, LANES))
    w1 = jnp.broadcast_to(route[:, 3:4], (tm, LANES))

    def emit(out_ref):
        for s in range(SUBLANES):
            cols = slice(s * LANES, (s + 1) * LANES)
            out_ref[:, cols] = (h_ref[:, cols] + w0 * _token_tile_chunk(buf.at[slot, 0], tm, s)
                                + w1 * _token_tile_chunk(buf.at[slot, 1], tm, s))
    pl.when(i < n_prompt_tiles)(lambda: emit(yp_ref))
    pl.when(i >= n_prompt_tiles)(lambda: emit(ys_ref))


def _combine(dest, y_sorted, h, route, *, tp, ts):
    tm = COMBINE_TM
    d = h.shape[1]
    t_all = tp + ts
    npt = tp // tm
    n_tiles = t_all // tm
    grid_spec = pltpu.PrefetchScalarGridSpec(
        num_scalar_prefetch=1,
        grid=(n_tiles,),
        in_specs=[pl.BlockSpec(memory_space=pl.ANY),
                  pl.BlockSpec((tm, d), lambda i, de: (i, 0)),
                  pl.BlockSpec((tm, LANES), lambda i, de: (i, 0))],
        out_specs=[pl.BlockSpec((tm, d), lambda i, de: (jnp.minimum(i, npt - 1), 0)),
                   pl.BlockSpec((tm, d), lambda i, de: (jnp.maximum(i - npt, 0), 0))],
        scratch_shapes=[pltpu.VMEM((2, 2, tm * SUBLANES, LANES), F32), pltpu.SemaphoreType.DMA((2,))])
    return pl.pallas_call(
        functools.partial(_combine_kernel, tm=tm, t_all=t_all, n_prompt_tiles=npt),
        grid_spec=grid_spec,
        out_shape=[jax.ShapeDtypeStruct((tp, d), F32), jax.ShapeDtypeStruct((ts, d), F32)],
        compiler_params=pltpu.CompilerParams(
            dimension_semantics=("arbitrary",),
            vmem_limit_bytes=VMEM_LIMIT_BYTES),
    )(dest, y_sorted, h, route)


def _routing_tables(route, *, tm, n_tiles):
    t_all = route.shape[0]
    e_flat = jnp.concatenate([route[:, 0], route[:, 1]]).astype(jnp.int32)
    onehot = (e_flat[:, None] == jnp.arange(N_EXPERTS, dtype=jnp.int32)[None, :]).astype(jnp.int32)
    csum = jnp.cumsum(onehot, axis=0)
    rank = jnp.sum(csum * onehot, axis=1) - 1
    counts = csum[-1]
    tiles_e = (counts + tm - 1) // tm
    tile_end = jnp.cumsum(tiles_e)
    tile_start = tile_end - tiles_e
    dest = (tile_start * tm)[e_flat] + rank
    token = jnp.arange(2 * t_all, dtype=jnp.int32) % t_all
    src = jnp.zeros((n_tiles * tm,), jnp.int32).at[dest].set(token, unique_indices=True)
    tile_expert = jnp.minimum(
        jnp.sum((jnp.arange(n_tiles, dtype=jnp.int32)[:, None] >= tile_end[None, :]).astype(jnp.int32), axis=1),
        N_EXPERTS - 1).astype(jnp.int32)
    n_used = tile_end[-1:].astype(jnp.int32)
    return tile_expert, src, n_used, dest.astype(jnp.int32)


def _pad_lanes(v, n=LANES, offset=0):
    out = jnp.zeros((1, n), F32)
    return out.at[0, offset:offset + v.shape[0]].set(v.astype(F32))


def kernel(x_prompt, x_sample, cache_conv_a, state_delta_a, cache_k_sb, cache_v_sb, norm1_g, w_in, conv_a_w, a_log, dt_bias, a_out_norm_g, sb_q_norm_g, sb_k_norm_g, w_branch_a, w_branch_b, w_out, norm2_g, w_group, b_group, w_expert_router, b_expert_router, w1, w3, w2):
    assert norm1_g.shape[0] == 1, "single-layer model"
    bp, lp, d = x_prompt.shape
    bs, ls, _ = x_sample.shape
    past = cache_k_sb.shape[2]
    tp, ts = bp * lp, bs * ls
    nh, hd = N_HEADS, HEAD_DIM
    a_conv = 3 * nh * hd

    wi = w_in[0]
    c_z = a_conv + nh * hd
    c_ba = c_z + 2 * nh
    w_main = jnp.concatenate([wi[:, :c_z], wi[:, c_ba:]], axis=1).astype(BF16)
    w_ba = jnp.pad(wi[:, c_z:c_ba], ((0, 0), (0, LANES - 2 * nh))).astype(BF16)
    alog_pad = _pad_lanes(a_log[0], offset=nh)
    dtb_pad = _pad_lanes(dt_bias[0], offset=nh)
    x_p = x_prompt.reshape(tp, d)
    x_s = x_sample.reshape(ts, d)

    proj_w = (norm1_g, w_main, w_ba, sb_q_norm_g, sb_k_norm_g, alog_pad, dtb_pad)
    conv_in_p, z_p, q_p, k_p, v_p, gates_p, ba_p = _proj(x_p, *proj_w)
    conv_in_s, z_s, q_s, k_s, v_s, gates_s, ba_s = _proj(x_s, *proj_w)

    conv_w = conv_a_w[0]
    hist_p = jnp.zeros((bp, SUBLANES, a_conv), F32)
    s0_p = jnp.zeros((bp, nh, hd, hd), F32)
    oa_p, delta_p = _delta(conv_in_p, 0, hist_p, conv_w, ba_p, z_p, s0_p, a_out_norm_g,
                           batch=bp, length=lp, valid_chunks=DELTA_ROWS // CHUNK)
    pad_rows = DELTA_ROWS - ls

    def pad_stream(a):
        a = a.reshape(bs, ls, a.shape[-1])
        return jnp.pad(a, ((0, 0), (0, pad_rows), (0, 0))).reshape(bs * DELTA_ROWS, a.shape[-1])

    hist_s = jnp.pad(cache_conv_a[0], ((0, 0), (SUBLANES - (CONV_W - 1), 0), (0, 0)))
    oa_s_pad, delta_s = _delta(pad_stream(conv_in_s), 0, hist_s, conv_w, pad_stream(ba_s), pad_stream(z_s),
                               state_delta_a[0], a_out_norm_g,
                               batch=bs, length=DELTA_ROWS, valid_chunks=ls // CHUNK)
    oa_s = oa_s_pad.reshape(bs, DELTA_ROWS, nh * hd)[:, :ls].reshape(ts, nh * hd)

    bk = 256
    ob_p = _sb_attn(q_p, 0, k_p.reshape(bp, lp, nh * hd), v_p.reshape(bp, lp, nh * hd),
                    batch=bp, q_len=lp, tq=2 * bk, bk=bk, q_start=0)
    lk_s = past + ls
    lk_pad = -(-lk_s // bk) * bk
    k_all = jnp.concatenate([cache_k_sb[0].reshape(bs, past, nh * hd), k_s.reshape(bs, ls, nh * hd)], axis=1)
    v_all = jnp.concatenate([cache_v_sb[0].reshape(bs, past, nh * hd), v_s.reshape(bs, ls, nh * hd)], axis=1)
    k_all = jnp.pad(k_all, ((0, 0), (0, lk_pad - lk_s), (0, 0)))
    v_all = jnp.pad(v_all, ((0, 0), (0, lk_pad - lk_s), (0, 0)))
    ob_s = _sb_attn(q_s, 0, k_all, v_all, batch=bs, q_len=ls, tq=ls, bk=bk, q_start=past)

    w_router = jnp.concatenate(
        [w_group[0], jnp.moveaxis(w_expert_router[0], 0, 1).reshape(d, N_EXPERTS)], axis=1)
    w_router = jnp.pad(w_router, ((0, 0), (0, LANES - N_GROUPS - N_EXPERTS)))
    wr_hi = w_router.astype(BF16)
    wr_lo = (w_router - wr_hi.astype(F32)).astype(BF16)
    b_router = _pad_lanes(jnp.concatenate([b_group[0], b_expert_router[0].reshape(N_EXPERTS)]))
    h_all, hn_all, route = _merge(oa_p, oa_s, ob_p, ob_s, gates_p, gates_s, x_p, x_s,
                                  w_branch_a[0].astype(BF16), w_branch_b[0].astype(BF16),
                                  w_out[0].astype(BF16), norm2_g, wr_hi, wr_lo, b_router)

    t_all = tp + ts
    n_tiles = (2 * t_all + N_EXPERTS * (MOE_TM - 1)) // MOE_TM + 1
    tile_expert, src, n_used, dest = _routing_tables(route, tm=MOE_TM, n_tiles=n_tiles)
    ew1 = w1[0].reshape(N_EXPERTS, d, D_EXPERT)
    ew3 = w3[0].reshape(N_EXPERTS, d, D_EXPERT)
    ew2 = w2[0].reshape(N_EXPERTS, D_EXPERT, d)
    y_sorted = _moe_ffn(tile_expert, src, n_used, hn_all, ew1, ew3, ew2, n_tiles=n_tiles)
    y_p, y_s = _combine(dest, y_sorted, h_all, route, tp=tp, ts=ts)

    def last_rows(conv_in, n, length):
        return jnp.stack([conv_in[(b + 1) * length - (CONV_W - 1):(b + 1) * length]
                          for b in range(n)]).astype(F32)
    conv_p = last_rows(conv_in_p, bp, lp)
    conv_s = last_rows(conv_in_s, bs, ls)
    return (y_p.reshape(bp, lp, d), y_s.reshape(bs, ls, d),
            conv_p[None], delta_p[None],
            k_p.reshape(1, bp, lp, nh, hd), v_p.reshape(1, bp, lp, nh, hd),
            conv_s[None], delta_s[None],
            k_s.reshape(1, bs, ls, nh, hd), v_s.reshape(1, bs, ls, nh, hd))
```

```python
import functools

import jax
import jax.numpy as jnp
from jax import lax
from jax.experimental import pallas as pl
from jax.experimental.pallas import tpu as pltpu

F32 = jnp.float32
BF16 = jnp.bfloat16

EPS = 1e-6
CHUNK = 64
HEAD_DIM = 128
N_HEADS = 8
D_MODEL = 1024
CONV_W = 4
N_GROUPS = 4
EXPERTS_PER_GROUP = 8
N_EXPERTS = N_GROUPS * EXPERTS_PER_GROUP
D_EXPERT = 512
LANES = 128
SUBLANES = 8
VMEM_LIMIT_BYTES = 56 * 1024 * 1024

PROJ_TM = 512
MERGE_TM = 256
MOE_TM = 256
COMBINE_TM = 256
DELTA_ROWS = 2 * CHUNK
_SB_DEAD = 110.0
_W_CHUNKS = 4


def _dot(a, b):
    return jnp.dot(a, b, preferred_element_type=F32)


def _dot_nt(a, b):
    return lax.dot_general(a, b, (((1,), (1,)), ((), ())), preferred_element_type=F32)


def _split_bf16(x):
    hi = x.astype(BF16)
    lo = (x - hi.astype(F32)).astype(BF16)
    return hi, lo


def _sigmoid(x):
    return 1.0 / (1.0 + jnp.exp(-x))


def _softplus(x):
    return jnp.maximum(x, 0.0) + jnp.log(1.0 + jnp.exp(-jnp.abs(x)))


def _rms_rows(x, g):
    ms = jnp.mean(x * x, axis=-1, keepdims=True)
    return x * lax.rsqrt(ms + EPS) * g


assert D_MODEL == SUBLANES * LANES


def _token_rows(t):
    return pl.ds(pl.multiple_of(t * SUBLANES, SUBLANES), SUBLANES)


def _to_token_tiles(ref, val):
    for s in range(SUBLANES):
        ref[pl.ds(s, val.shape[0], stride=SUBLANES), :] = val[:, s * LANES:(s + 1) * LANES]


def _token_tile_chunk(ref, n, s):
    return ref[pl.ds(s, n, stride=SUBLANES), :]


def _from_token_tiles(ref, n):
    return jnp.concatenate([_token_tile_chunk(ref, n, s) for s in range(SUBLANES)], axis=1)


def _proj_kernel(x_ref, g1_ref, w_ref, wba_ref, qg_ref, kg_ref, alog_ref, dtb_ref,
                 conv_ref, z_ref, qb_ref, k_ref, v_ref, gate_ref, ba_ref):
    d = D_MODEL
    xn = _rms_rows(x_ref[...], g1_ref[...]).astype(BF16)

    ba = _dot(xn, wba_ref[...])
    lane = lax.broadcasted_iota(jnp.int32, ba.shape, 1)
    g = -jnp.exp(alog_ref[...]) * _softplus(ba + dtb_ref[...])
    ba_ref[...] = jnp.where(lane < N_HEADS, _sigmoid(ba), g)

    def seg(c):
        return _dot(xn, w_ref[:, c * d:(c + 1) * d])

    def head_rms_store(out_ref, val, g):
        for h in range(N_HEADS):
            sl = slice(h * HEAD_DIM, (h + 1) * HEAD_DIM)
            out_ref[:, sl] = _rms_rows(val[:, sl], g).astype(out_ref.dtype)

    for c in range(3):
        conv_ref[:, c * d:(c + 1) * d] = seg(c).astype(conv_ref.dtype)
    z_ref[...] = seg(3).astype(z_ref.dtype)
    head_rms_store(qb_ref, seg(4), qg_ref[...])
    head_rms_store(k_ref, seg(5), kg_ref[...])
    v_ref[...] = seg(6)
    for c in range(2):
        gate_ref[:, c * d:(c + 1) * d] = _sigmoid(seg(7 + c)).astype(gate_ref.dtype)


def _proj(x, g1, w_main, w_ba, qg, kg, alog_pad, dtb_pad):
    t, d = x.shape
    tm = PROJ_TM
    const = lambda i: (0, 0)
    rows = lambda i: (i, 0)

    def resident(shape):
        return pl.BlockSpec(shape, const, pipeline_mode=pl.Buffered(1))

    in_specs = [
        pl.BlockSpec((tm, d), rows),
        resident((1, d)),
        resident(w_main.shape),
        resident((d, LANES)),
        resident((1, HEAD_DIM)),
        resident((1, HEAD_DIM)),
        resident((1, LANES)),
        resident((1, LANES)),
    ]
    widths_dtypes = [(3 * d, BF16),
                     (d, BF16),
                     (d, BF16),
                     (d, F32),
                     (d, F32),
                     (2 * d, BF16),
                     (LANES, F32)]
    return pl.pallas_call(
        _proj_kernel,
        grid=(t // tm,),
        in_specs=in_specs,
        out_specs=[pl.BlockSpec((tm, w), rows) for w, _ in widths_dtypes],
        out_shape=[jax.ShapeDtypeStruct((t, w), dt) for w, dt in widths_dtypes],
        compiler_params=pltpu.CompilerParams(
            dimension_semantics=("arbitrary",),
            vmem_limit_bytes=VMEM_LIMIT_BYTES),
    )(x, g1, w_main, w_ba, qg, kg, alog_pad, dtb_pad)


def _conv_silu(cur, prev8, w):
    acc = cur * w[CONV_W - 1:CONV_W, :]
    rows8 = lax.broadcasted_iota(jnp.int32, prev8.shape, 0)
    for k in range(1, CONV_W):
        sh = pltpu.roll(cur, k, axis=0)
        ph = pltpu.roll(prev8, k, axis=0)
        head = jnp.where(rows8 < k, ph, sh[0:SUBLANES])
        shifted = jnp.concatenate([head, sh[SUBLANES:]], axis=0)
        acc = acc + shifted * w[CONV_W - 1 - k:CONV_W - k, :]
    return acc * _sigmoid(acc)


def _l2_rows(x):
    return x * lax.rsqrt(jnp.sum(x * x, axis=-1, keepdims=True) + EPS)


def _delta_kernel(conv_ref, hist_ref, cw_ref, ba_ref, z_ref, s0_ref, ng_ref, o_ref, sout_ref,
                  s_ref, tail_ref, *, valid_chunks):
    tt = pl.program_id(1)
    R = DELTA_ROWS
    nh, hd = N_HEADS, HEAD_DIM
    heads = range(nh)

    @pl.when(tt == 0)
    def _():
        s_ref[...] = s0_ref[0]
        tail_ref[...] = hist_ref[0]

    raw = conv_ref[...].astype(F32)
    act = _conv_silu(raw, tail_ref[...], cw_ref[...])
    tail_ref[...] = raw[R - SUBLANES:, :]
    Q = [_l2_rows(act[:, h * hd:(h + 1) * hd]) * (hd ** -0.5) for h in heads]
    K = [_l2_rows(act[:, (nh + h) * hd:(nh + h + 1) * hd]) for h in heads]
    V = [act[:, (2 * nh + h) * hd:(2 * nh + h + 1) * hd] for h in heads]

    row = lax.broadcasted_iota(jnp.int32, (R, R), 0)
    col = lax.broadcasted_iota(jnp.int32, (R, R), 1)
    log2 = lambda n: n.bit_length() - 1
    same_chunk = (row >> log2(CHUNK)) == (col >> log2(CHUNK))
    incl = jnp.logical_and(same_chunk, col <= row)
    strict = jnp.logical_and(same_chunk, col < row)
    cs_mat = jnp.where(incl, 1.0, 0.0).astype(BF16)
    eye = jnp.where(row == col, 1.0, 0.0)

    def level_mask(s):
        same = (row >> log2(2 * s)) == (col >> log2(2 * s))
        return jnp.logical_and(jnp.logical_and(same, ((row >> log2(s)) & 1) == 1), ((col >> log2(s)) & 1) == 0)

    ba = ba_ref[...]
    ba_hi, ba_lo = _split_bf16(ba)
    G_all = _dot(cs_mat, ba_hi) + _dot(cs_mat, ba_lo)
    G_all_t = G_all.T

    beta_b = [jnp.broadcast_to(ba[:, h:h + 1], (R, R)) for h in heads]
    Gb = [jnp.broadcast_to(G_all[:, nh + h:nh + h + 1], (R, R)) for h in heads]
    decay = []
    for h in heads:
        g_row = jnp.broadcast_to(G_all_t[nh + h:nh + h + 1, :], (R, R))
        decay.append(jnp.where(incl, jnp.exp(jnp.where(incl, Gb[h] - g_row, 0.0)), 0.0))
    Kb = [K[h].astype(BF16) for h in heads]
    kq = [_dot_nt(jnp.concatenate([Kb[h], Q[h].astype(BF16)], axis=0), Kb[h]) for h in heads]
    Lm = [jnp.where(strict, beta_b[h] * kq[h][:R] * decay[h], 0.0) for h in heads]
    a_qk = [(kq[h][R:] * decay[h]).astype(BF16) for h in heads]

    T = [eye - jnp.where(level_mask(1), Lm[h], 0.0) for h in heads]
    s = 2
    while s < CHUNK:
        mask = level_mask(s)
        Tb = [T[h].astype(BF16) for h in heads]
        X = [_dot(jnp.where(mask, Lm[h], 0.0).astype(BF16), Tb[h]).astype(BF16) for h in heads]
        T = [T[h] - _dot(Tb[h], X[h]) for h in heads]
        s *= 2

    expG = [jnp.exp(Gb[h]) for h in heads]
    uw = [_dot(T[h].astype(BF16),
               jnp.concatenate([beta_b[h] * V[h], beta_b[h] * expG[h] * K[h]], axis=1).astype(BF16))
          for h in heads]
    g_last = [jnp.concatenate([jnp.broadcast_to(Gb[h][CHUNK - 1:CHUNK, :], (CHUNK, R)),
                               jnp.broadcast_to(Gb[h][R - 1:R, :], (CHUNK, R))], axis=0) for h in heads]
    k_tail_t = [(K[h] * jnp.exp(g_last[h] - Gb[h])).T.astype(BF16) for h in heads]

    zeros_half = jnp.zeros((CHUNK, hd), F32)
    S = [s_ref[h] for h in heads]
    o_rows = []
    for c in range(2):
        if c >= valid_chunks:
            o_rows.append(jnp.zeros((CHUNK, nh * hd), F32))
            continue
        rc = slice(c * CHUNK, (c + 1) * CHUNK)
        lhs_s = [jnp.concatenate([uw[h][rc, hd:], Q[h][rc]], axis=0).astype(BF16) for h in heads]
        lhs_u = [jnp.concatenate([a_qk[h][rc], k_tail_t[h]], axis=0) for h in heads]
        ws = [_dot(lhs_s[h], S[h].astype(BF16)) for h in heads]
        U = [uw[h][rc, :hd] - ws[h][:CHUNK] for h in heads]
        Ucat = [(jnp.concatenate([U[h], zeros_half], axis=0) if c == 0
                 else jnp.concatenate([zeros_half, U[h]], axis=0)).astype(BF16) for h in heads]
        au = [_dot(lhs_u[h], Ucat[h]) for h in heads]
        o_rows.append(jnp.concatenate(
            [expG[h][rc] * ws[h][CHUNK:] + au[h][:CHUNK] for h in heads], axis=1))
        S = [jnp.exp(g_last[h][rc][0:1, :]) * S[h] + au[h][CHUNK:] for h in heads]
    for h in heads:
        s_ref[h] = S[h]
    o_raw = jnp.concatenate(o_rows, axis=0)
    z = z_ref[...].astype(F32)
    gate = z * _sigmoid(z)
    for h in heads:
        sl = slice(h * hd, (h + 1) * hd)
        o_ref[:, sl] = _rms_rows(o_raw[:, sl], ng_ref[...]) * gate[:, sl]

    @pl.when(tt == pl.num_programs(1) - 1)
    def _():
        for h in heads:
            sout_ref[0, h] = S[h]


def _delta(conv_in, row_block_off, hist8, conv_w, ba, z, s0, norm_g, *, batch, length, valid_chunks):
    lt = DELTA_ROWS
    nt = length // lt
    nh, hd = N_HEADS, HEAD_DIM
    c_conv = conv_in.shape[1]

    def rows(b, t):
        return (row_block_off + b * nt + t, 0)

    in_specs = [pl.BlockSpec((lt, c_conv), rows),
                pl.BlockSpec((1, SUBLANES, c_conv), lambda b, t: (b, 0, 0)),
                pl.BlockSpec((CONV_W, c_conv), lambda b, t: (0, 0)),
                pl.BlockSpec((lt, LANES), rows),
                pl.BlockSpec((lt, nh * hd), rows),
                pl.BlockSpec((1, nh, hd, hd), lambda b, t: (b, 0, 0, 0)),
                pl.BlockSpec((1, hd), lambda b, t: (0, 0))]
    out_specs = [pl.BlockSpec((lt, nh * hd), lambda b, t: (b * nt + t, 0)),
                 pl.BlockSpec((1, nh, hd, hd), lambda b, t: (b, 0, 0, 0))]
    out_shape = [jax.ShapeDtypeStruct((batch * length, nh * hd), F32),
                 jax.ShapeDtypeStruct((batch, nh, hd, hd), F32)]
    return pl.pallas_call(
        functools.partial(_delta_kernel, valid_chunks=valid_chunks),
        grid=(batch, nt),
        in_specs=in_specs,
        out_specs=out_specs,
        out_shape=out_shape,
        scratch_shapes=[pltpu.VMEM((nh, hd, hd), F32),
                        pltpu.VMEM((SUBLANES, c_conv), F32)],
        compiler_params=pltpu.CompilerParams(
            dimension_semantics=("arbitrary", "arbitrary"),
            vmem_limit_bytes=VMEM_LIMIT_BYTES),
    )(conv_in, hist8, conv_w, ba, z, s0, norm_g)


def _sb_kernel(q_ref, k_ref, v_ref, later2_ref, bias_ref, o_ref, zz_ref, logb_ref, cs_ref, w_ref,
               *, tq, bk, q_start):
    i = pl.program_id(2)
    q = (q_ref[...].astype(F32) * (HEAD_DIM ** -0.5)).astype(BF16)
    q_pos0 = q_start + i * tq
    n_kb = (q_pos0 + tq - 1 + bk - 1) // bk
    later2 = later2_ref[...]

    def key_start(j):
        return pl.multiple_of((n_kb - 1 - jnp.clip(j, 0, n_kb - 1)) * bk, bk)

    def scores(j):
        return _dot_nt(q, k_ref[0, pl.ds(key_start(j), bk), :].astype(BF16))

    def stay_terms(zz):
        s = jnp.maximum(zz, 0.0) + jnp.log(1.0 + jnp.exp(-jnp.abs(zz)))
        hi, lo = _split_bf16(s)
        cs = _dot(jnp.concatenate([hi, lo], axis=1), later2)
        return zz - s, cs, jnp.sum(s, axis=1, keepdims=True)

    def weights(acc):
        return jnp.exp(logb_ref[...] - cs_ref[...] - acc).astype(BF16)

    def weighted_values(j, w):
        return _dot(w, v_ref[0, pl.ds(key_start(j), bk), :].astype(BF16))

    logb_ref[...], cs_ref[...], rs = stay_terms(scores(0) + bias_ref[0])
    zz_ref[...] = scores(1) + bias_ref[1] if tq > bk else scores(1)
    w_ref[...] = jnp.zeros_like(w_ref)
    o_ref[...] = jnp.zeros_like(o_ref)

    def live(acc):
        return (jnp.min(acc) < _SB_DEAD).astype(jnp.int32)

    def cond(carry):
        j, go, _, _ = carry
        return jnp.logical_and(j < n_kb - 1, go > 0)

    def body(carry):
        j, _, acc, rs = carry
        acc_next = acc + rs
        go_next = live(acc_next)
        o_ref[...] += weighted_values(j - 1, w_ref[...])
        w_ref[...] = weights(acc)
        zz_next = zz_ref[...]
        zz_ref[...] = scores(j + 2)
        logb_ref[...], cs_ref[...], rs_next = stay_terms(zz_next)
        return j + 1, go_next, acc_next, rs_next

    j, go, acc, _ = lax.while_loop(cond, body, (jnp.int32(0), jnp.int32(1), jnp.zeros((tq, 1), F32), rs))
    o_ref[...] += weighted_values(j - 1, w_ref[...])

    @pl.when(go > 0)
    def _():
        o_ref[...] += weighted_values(j, weights(acc))


def _sb_attn(q, q_row_block_off, k, v, *, batch, q_len, tq, bk, q_start):
    assert q_start % bk == 0 and (tq in (bk, 2 * bk) or (bk % tq == 0 and q_len == tq))
    nq = q_len // tq
    lk = k.shape[1]
    kv_spec = pl.BlockSpec((1, lk, HEAD_DIM), lambda b, h, i: (b, 0, h))
    ii = jnp.arange(bk, dtype=jnp.int32)
    later = (ii[:, None] > ii[None, :]).astype(BF16)
    later2 = jnp.concatenate([later, later], axis=0)
    n_diag = max(1, tq // bk)
    newest_start = ((tq - 1 + bk - 1) // bk - 1) * bk
    k_rel = (newest_start - bk * jnp.arange(n_diag, dtype=jnp.int32))[:, None, None] + ii[None, None, :]
    q_rel = jnp.arange(tq, dtype=jnp.int32)[None, :, None]
    bias = jnp.where(k_rel < q_rel, 0.0, -1e30).astype(F32)

    def const(shape):
        return pl.BlockSpec(shape, lambda b, h, i: (0,) * len(shape), pipeline_mode=pl.Buffered(1))

    return pl.pallas_call(
        functools.partial(_sb_kernel, tq=tq, bk=bk, q_start=q_start),
        grid=(batch, N_HEADS, nq),
        in_specs=[pl.BlockSpec((tq, HEAD_DIM), lambda b, h, i: (q_row_block_off + b * nq + i, h)),
                  kv_spec, kv_spec, const(later2.shape), const(bias.shape)],
        out_specs=pl.BlockSpec((tq, HEAD_DIM), lambda b, h, i: (b * nq + i, h)),
        out_shape=jax.ShapeDtypeStruct((batch * q_len, N_HEADS * HEAD_DIM), F32),
        scratch_shapes=[pltpu.VMEM((tq, bk), F32)] * 3 + [pltpu.VMEM((tq, bk), BF16)],
        compiler_params=pltpu.CompilerParams(
            dimension_semantics=("arbitrary", "arbitrary", "arbitrary"),
            vmem_limit_bytes=VMEM_LIMIT_BYTES),
    )(q, k, v, later2, bias)


def _merge_kernel(oap_ref, oas_ref, obp_ref, obs_ref, gp_ref, gs_ref, xp_ref, xs_ref,
                  wa_ref, wb_ref, wo_ref, g2_ref, wrh_ref, wrl_ref, br_ref,
                  h_ref, hn_ref, route_ref, *, n_prompt_tiles):
    i = pl.program_id(0)

    def run(oa_ref, ob_ref, gate_ref, x_ref):
        ga = gate_ref[:, :D_MODEL].astype(F32)
        gb = gate_ref[:, D_MODEL:].astype(F32)
        merged = ga * _dot(oa_ref[...].astype(BF16), wa_ref[...]) + gb * _dot(ob_ref[...].astype(BF16), wb_ref[...])
        hh = x_ref[...] + _dot(merged.astype(BF16), wo_ref[...])
        h_ref[...] = hh
        hn = _rms_rows(hh, g2_ref[...])
        _to_token_tiles(hn_ref, hn)
        hi, lo = _split_bf16(hn)
        logits = _dot(hi, wrh_ref[...]) + _dot(lo, wrh_ref[...]) + _dot(hi, wrl_ref[...]) + br_ref[...]
        lane_i = lax.broadcasted_iota(jnp.int32, logits.shape, 1)
        lane = lane_i.astype(F32)
        neg = jnp.float32(-jnp.inf)
        big = jnp.float32(LANES)
        is_g = lane_i < N_GROUPS
        lg = jnp.where(is_g, logits, neg)
        g_max = jnp.max(lg, axis=1, keepdims=True)
        g_idx = jnp.min(jnp.where(lg == g_max, lane, big), axis=1, keepdims=True)
        p_top = 1.0 / jnp.sum(jnp.where(is_g, jnp.exp(lg - g_max), 0.0), axis=1, keepdims=True)
        e_lane = lane_i - N_GROUPS
        lane_group = (e_lane >> (EXPERTS_PER_GROUP.bit_length() - 1)).astype(F32)
        sel = jnp.logical_and(jnp.logical_and(e_lane >= 0, e_lane < N_EXPERTS), lane_group == g_idx)
        le = jnp.where(sel, logits, neg)
        m1 = jnp.max(le, axis=1, keepdims=True)
        i1 = jnp.min(jnp.where(le == m1, lane, big), axis=1, keepdims=True)
        le2 = jnp.where(lane == i1, neg, le)
        m2 = jnp.max(le2, axis=1, keepdims=True)
        i2 = jnp.min(jnp.where(le2 == m2, lane, big), axis=1, keepdims=True)
        e2 = jnp.exp(m2 - m1)
        w1 = p_top / (1.0 + e2)
        w2 = p_top * e2 / (1.0 + e2)
        route = jnp.where(lane_i == 0, i1 - N_GROUPS,
                          jnp.where(lane_i == 1, i2 - N_GROUPS,
                                    jnp.where(lane_i == 2, w1, jnp.where(lane_i == 3, w2, 0.0))))
        route_ref[...] = route

    pl.when(i < n_prompt_tiles)(lambda: run(oap_ref, obp_ref, gp_ref, xp_ref))
    pl.when(i >= n_prompt_tiles)(lambda: run(oas_ref, obs_ref, gs_ref, xs_ref))


def _merge(oa_p, oa_s, ob_p, ob_s, gates_p, gates_s, x_p, x_s, wa, wb, wo, g2, wr_hi, wr_lo, br):
    tp, d = x_p.shape
    ts = x_s.shape[0]
    tm = MERGE_TM
    npt = tp // tm
    n_tiles = npt + ts // tm
    t_all = tp + ts
    p_spec = pl.BlockSpec((tm, d), lambda i: (jnp.minimum(i, npt - 1), 0))
    s_spec = pl.BlockSpec((tm, d), lambda i: (jnp.maximum(i - npt, 0), 0))
    const = lambda i: (0, 0)
    w_spec = pl.BlockSpec((d, d), const)
    in_specs = [p_spec, s_spec, p_spec, s_spec,
                pl.BlockSpec((tm, 2 * d), lambda i: (jnp.minimum(i, npt - 1), 0)),
                pl.BlockSpec((tm, 2 * d), lambda i: (jnp.maximum(i - npt, 0), 0)),
                p_spec, s_spec, w_spec, w_spec, w_spec,
                pl.BlockSpec((1, d), const),
                pl.BlockSpec((d, LANES), const), pl.BlockSpec((d, LANES), const),
                pl.BlockSpec((1, LANES), const)]
    out_specs = [pl.BlockSpec((tm, d), lambda i: (i, 0)),
                 pl.BlockSpec((tm * SUBLANES, LANES), lambda i: (i, 0)),
                 pl.BlockSpec((tm, LANES), lambda i: (i, 0))]
    out_shape = [jax.ShapeDtypeStruct((t_all, d), F32),
                 jax.ShapeDtypeStruct((t_all * SUBLANES, LANES), F32),
                 jax.ShapeDtypeStruct((t_all, LANES), F32)]
    return pl.pallas_call(
        functools.partial(_merge_kernel, n_prompt_tiles=npt),
        grid=(n_tiles,),
        in_specs=in_specs,
        out_specs=out_specs,
        out_shape=out_shape,
        compiler_params=pltpu.CompilerParams(
            dimension_semantics=("arbitrary",),
            vmem_limit_bytes=VMEM_LIMIT_BYTES),
    )(oa_p, oa_s, ob_p, ob_s, gates_p, gates_s, x_p, x_s, wa, wb, wo, g2, wr_hi, wr_lo, br)


def _moe_kernel(te_ref, nxt_ref, wslot_ref, src_ref, nused_ref, hn_hbm, w1_hbm, w3_hbm, w2_hbm, y_ref,
                xbuf, wf1, wf3, wf2, w1b, w3b, w2b, sem, wsem, *, tm):
    i = pl.program_id(0)
    n_used = nused_ref[0]
    slot = i % 2

    def weight_copies(e, ws):
        out = []
        for hbm, buf in ((w1_hbm, wf1), (w3_hbm, wf3), (w2_hbm, wf2)):
            rows = hbm.shape[1] // _W_CHUNKS
            for c in range(_W_CHUNKS):
                part = pl.ds(c * rows, rows)
                out.append(pltpu.make_async_copy(hbm.at[e, part], buf.at[ws, part], wsem.at[ws]))
        return out

    def token_copy(tile, sl, r):
        tok = src_ref[tile * tm + r]
        return pltpu.make_async_copy(hn_hbm.at[_token_rows(tok)], xbuf.at[sl, _token_rows(r)], sem.at[sl])

    def start_gather(tile, sl):
        def issue(r8, c):
            for k in range(SUBLANES):
                token_copy(tile, sl, r8 * SUBLANES + k).start()
            return c
        lax.fori_loop(0, tm // SUBLANES, issue, 0)

    def wait_gather(sl):
        pltpu.make_async_copy(hn_hbm.at[pl.ds(0, tm * SUBLANES)], xbuf.at[sl], sem.at[sl]).wait()

    @pl.when(jnp.logical_and(i == 0, n_used > 0))
    def _():
        start_gather(0, 0)

    @pl.when(i + 1 < n_used)
    def _():
        start_gather(i + 1, 1 - slot)

    @pl.when(i < n_used)
    def _():
        prev = te_ref[jnp.maximum(i - 1, 0)]
        fresh = jnp.logical_or(i == 0, te_ref[i] != prev)
        ws = wslot_ref[i]

        @pl.when(i == 0)
        def _():
            for cp in weight_copies(te_ref[0], ws):
                cp.start()

        @pl.when(fresh)
        def _():
            for cp in weight_copies(te_ref[i], ws):
                cp.wait()
            w1b[...] = wf1[ws].astype(BF16)
            w3b[...] = wf3[ws].astype(BF16)
            w2b[...] = wf2[ws].astype(BF16)

            @pl.when(nxt_ref[i] >= 0)
            def _():
                for cp in weight_copies(nxt_ref[i], 1 - ws):
                    cp.start()

        wait_gather(slot)
        x = _from_token_tiles(xbuf.at[slot], tm).astype(BF16)
        a = _dot(x, w1b[...])
        hid = (a * _sigmoid(a)) * _dot(x, w3b[...])
        _to_token_tiles(y_ref, _dot(hid.astype(BF16), w2b[...]))

    @pl.when(i >= n_used)
    def _():
        y_ref[...] = jnp.zeros_like(y_ref)


def _moe_ffn(tile_expert, next_expert, weight_slot, src, n_used, hn_tiles, w1, w3, w2, *, n_tiles):
    tm = MOE_TM
    d = w1.shape[1]
    f = w1.shape[2]
    any_spec = pl.BlockSpec(memory_space=pl.ANY)
    grid_spec = pltpu.PrefetchScalarGridSpec(
        num_scalar_prefetch=5,
        grid=(n_tiles,),
        in_specs=[any_spec, any_spec, any_spec, any_spec],
        out_specs=pl.BlockSpec((tm * SUBLANES, LANES), lambda i, *_: (i, 0)),
        scratch_shapes=[pltpu.VMEM((2, tm * SUBLANES, LANES), F32),
                        pltpu.VMEM((2, d, f), F32), pltpu.VMEM((2, d, f), F32), pltpu.VMEM((2, f, d), F32),
                        pltpu.VMEM((d, f), BF16), pltpu.VMEM((d, f), BF16), pltpu.VMEM((f, d), BF16),
                        pltpu.SemaphoreType.DMA((2,)), pltpu.SemaphoreType.DMA((2,))])
    return pl.pallas_call(
        functools.partial(_moe_kernel, tm=tm),
        grid_spec=grid_spec,
        out_shape=jax.ShapeDtypeStruct((n_tiles * tm * SUBLANES, LANES), F32),
        compiler_params=pltpu.CompilerParams(
            dimension_semantics=("arbitrary",),
            vmem_limit_bytes=VMEM_LIMIT_BYTES),
    )(tile_expert, next_expert, weight_slot, src, n_used, hn_tiles, w1, w3, w2)


def _combine_kernel(dest_ref, ys_hbm, h_ref, route_ref, yp_ref, ys_ref, buf, sem, *, tm, t_all, n_prompt_tiles):
    i = pl.program_id(0)
    n = pl.num_programs(0)
    slot = i % 2

    def start_gather(tile, sl):
        def issue(r8, c):
            for j in range(SUBLANES):
                r = r8 * SUBLANES + j
                for k in range(2):
                    row = dest_ref[k * t_all + tile * tm + r]
                    pltpu.make_async_copy(ys_hbm.at[_token_rows(row)], buf.at[sl, k, _token_rows(r)],
                                          sem.at[sl]).start()
            return c
        lax.fori_loop(0, tm // SUBLANES, issue, 0)

    def wait_gather(sl):
        for k in range(2):
            pltpu.make_async_copy(ys_hbm.at[pl.ds(0, tm * SUBLANES)], buf.at[sl, k], sem.at[sl]).wait()

    @pl.when(i == 0)
    def _():
        start_gather(0, 0)

    @pl.when(i + 1 < n)
    def _():
        start_gather(i + 1, 1 - slot)

    wait_gather(slot)
    route = route_ref[...]
    w0 = jnp.broadcast_to(route[:, 2:3], (tm, LANES))
    w1 = jnp.broadcast_to(route[:, 3:4], (tm, LANES))

    def emit(out_ref):
        for s in range(SUBLANES):
            cols = slice(s * LANES, (s + 1) * LANES)
            out_ref[:, cols] = (h_ref[:, cols] + w0 * _token_tile_chunk(buf.at[slot, 0], tm, s)
                                + w1 * _token_tile_chunk(buf.at[slot, 1], tm, s))
    pl.when(i < n_prompt_tiles)(lambda: emit(yp_ref))
    pl.when(i >= n_prompt_tiles)(lambda: emit(ys_ref))


def _combine(dest, y_sorted, h, route, *, tp, ts):
    tm = COMBINE_TM
    d = h.shape[1]
    t_all = tp + ts
    npt = tp // tm
    n_tiles = t_all // tm
    grid_spec = pltpu.PrefetchScalarGridSpec(
        num_scalar_prefetch=1,
        grid=(n_tiles,),
        in_specs=[pl.BlockSpec(memory_space=pl.ANY),
                  pl.BlockSpec((tm, d), lambda i, de: (i, 0)),
                  pl.BlockSpec((tm, LANES), lambda i, de: (i, 0))],
        out_specs=[pl.BlockSpec((tm, d), lambda i, de: (jnp.minimum(i, npt - 1), 0)),
                   pl.BlockSpec((tm, d), lambda i, de: (jnp.maximum(i - npt, 0), 0))],
        scratch_shapes=[pltpu.VMEM((2, 2, tm * SUBLANES, LANES), F32), pltpu.SemaphoreType.DMA((2,))])
    return pl.pallas_call(
        functools.partial(_combine_kernel, tm=tm, t_all=t_all, n_prompt_tiles=npt),
        grid_spec=grid_spec,
        out_shape=[jax.ShapeDtypeStruct((tp, d), F32), jax.ShapeDtypeStruct((ts, d), F32)],
        compiler_params=pltpu.CompilerParams(
            dimension_semantics=("arbitrary",),
            vmem_limit_bytes=VMEM_LIMIT_BYTES),
    )(dest, y_sorted, h, route)


def _routing_tables(route, *, tm, n_tiles):
    t_all = route.shape[0]
    e_flat = jnp.concatenate([route[:, 0], route[:, 1]]).astype(jnp.int32)
    onehot = (e_flat[:, None] == jnp.arange(N_EXPERTS, dtype=jnp.int32)[None, :]).astype(jnp.int32)
    csum = jnp.cumsum(onehot, axis=0)
    rank = jnp.sum(csum * onehot, axis=1) - 1
    counts = csum[-1]
    tiles_e = (counts + tm - 1) // tm
    tile_end = jnp.cumsum(tiles_e)
    tile_start = tile_end - tiles_e
    dest = (tile_start * tm)[e_flat] + rank
    token = jnp.arange(2 * t_all, dtype=jnp.int32) % t_all
    src = jnp.zeros((n_tiles * tm,), jnp.int32).at[dest].set(token, unique_indices=True)
    tile_expert = jnp.minimum(
        jnp.sum((jnp.arange(n_tiles, dtype=jnp.int32)[:, None] >= tile_end[None, :]).astype(jnp.int32), axis=1),
        N_EXPERTS - 1).astype(jnp.int32)
    n_used = tile_end[-1:].astype(jnp.int32)
    experts = jnp.arange(N_EXPERTS, dtype=jnp.int32)
    has_rows = counts > 0
    later_with_rows = jnp.logical_and(experts[None, :] > experts[:, None], has_rows[None, :])
    next_e = jnp.min(jnp.where(later_with_rows, experts[None, :], N_EXPERTS), axis=1)
    next_e = jnp.where(next_e == N_EXPERTS, -1, next_e).astype(jnp.int32)
    ordinal = jnp.cumsum(has_rows.astype(jnp.int32)) - 1
    next_expert = next_e[tile_expert]
    weight_slot = (ordinal[tile_expert] % 2).astype(jnp.int32)
    return tile_expert, next_expert, weight_slot, src, n_used, dest.astype(jnp.int32)


def _pad_lanes(v, n=LANES, offset=0):
    out = jnp.zeros((1, n), F32)
    return out.at[0, offset:offset + v.shape[0]].set(v.astype(F32))


def kernel(x_prompt, x_sample, cache_conv_a, state_delta_a, cache_k_sb, cache_v_sb, norm1_g, w_in, conv_a_w, a_log, dt_bias, a_out_norm_g, sb_q_norm_g, sb_k_norm_g, w_branch_a, w_branch_b, w_out, norm2_g, w_group, b_group, w_expert_router, b_expert_router, w1, w3, w2):
    assert norm1_g.shape[0] == 1, "single-layer model"
    bp, lp, d = x_prompt.shape
    bs, ls, _ = x_sample.shape
    past = cache_k_sb.shape[2]
    tp, ts = bp * lp, bs * ls
    nh, hd = N_HEADS, HEAD_DIM
    a_conv = 3 * nh * hd

    wi = w_in[0]
    c_z = a_conv + nh * hd
    c_ba = c_z + 2 * nh
    w_main = jnp.concatenate([wi[:, :c_z], wi[:, c_ba:]], axis=1).astype(BF16)
    w_ba = jnp.pad(wi[:, c_z:c_ba], ((0, 0), (0, LANES - 2 * nh))).astype(BF16)
    alog_pad = _pad_lanes(a_log[0], offset=nh)
    dtb_pad = _pad_lanes(dt_bias[0], offset=nh)
    x_p = x_prompt.reshape(tp, d)
    x_s = x_sample.reshape(ts, d)

    proj_w = (norm1_g, w_main, w_ba, sb_q_norm_g, sb_k_norm_g, alog_pad, dtb_pad)
    conv_in_p, z_p, q_p, k_p, v_p, gates_p, ba_p = _proj(x_p, *proj_w)
    conv_in_s, z_s, q_s, k_s, v_s, gates_s, ba_s = _proj(x_s, *proj_w)

    conv_w = conv_a_w[0]
    hist_p = jnp.zeros((bp, SUBLANES, a_conv), F32)
    s0_p = jnp.zeros((bp, nh, hd, hd), F32)
    oa_p, delta_p = _delta(conv_in_p, 0, hist_p, conv_w, ba_p, z_p, s0_p, a_out_norm_g,
                           batch=bp, length=lp, valid_chunks=DELTA_ROWS // CHUNK)
    pad_rows = DELTA_ROWS - ls

    def pad_stream(a):
        a = a.reshape(bs, ls, a.shape[-1])
        return jnp.pad(a, ((0, 0), (0, pad_rows), (0, 0))).reshape(bs * DELTA_ROWS, a.shape[-1])

    hist_s = jnp.pad(cache_conv_a[0], ((0, 0), (SUBLANES - (CONV_W - 1), 0), (0, 0)))
    oa_s_pad, delta_s = _delta(pad_stream(conv_in_s), 0, hist_s, conv_w, pad_stream(ba_s), pad_stream(z_s),
                               state_delta_a[0], a_out_norm_g,
                               batch=bs, length=DELTA_ROWS, valid_chunks=ls // CHUNK)
    oa_s = oa_s_pad.reshape(bs, DELTA_ROWS, nh * hd)[:, :ls].reshape(ts, nh * hd)

    bk = 256
    ob_p = _sb_attn(q_p, 0, k_p.reshape(bp, lp, nh * hd), v_p.reshape(bp, lp, nh * hd),
                    batch=bp, q_len=lp, tq=2 * bk, bk=bk, q_start=0)
    lk_s = past + ls
    lk_pad = -(-lk_s // bk) * bk
    k_all = jnp.concatenate([cache_k_sb[0].reshape(bs, past, nh * hd), k_s.reshape(bs, ls, nh * hd)], axis=1)
    v_all = jnp.concatenate([cache_v_sb[0].reshape(bs, past, nh * hd), v_s.reshape(bs, ls, nh * hd)], axis=1)
    k_all = jnp.pad(k_all, ((0, 0), (0, lk_pad - lk_s), (0, 0)))
    v_all = jnp.pad(v_all, ((0, 0), (0, lk_pad - lk_s), (0, 0)))
    ob_s = _sb_attn(q_s, 0, k_all, v_all, batch=bs, q_len=ls, tq=ls, bk=bk, q_start=past)

    w_router = jnp.concatenate(
        [w_group[0], jnp.moveaxis(w_expert_router[0], 0, 1).reshape(d, N_EXPERTS)], axis=1)
    w_router = jnp.pad(w_router, ((0, 0), (0, LANES - N_GROUPS - N_EXPERTS)))
    wr_hi = w_router.astype(BF16)
    wr_lo = (w_router - wr_hi.astype(F32)).astype(BF16)
    b_router = _pad_lanes(jnp.concatenate([b_group[0], b_expert_router[0].reshape(N_EXPERTS)]))
    h_all, hn_all, route = _merge(oa_p, oa_s, ob_p, ob_s, gates_p, gates_s, x_p, x_s,
                                  w_branch_a[0].astype(BF16), w_branch_b[0].astype(BF16),
                                  w_out[0].astype(BF16), norm2_g, wr_hi, wr_lo, b_router)

    t_all = tp + ts
    n_tiles = (2 * t_all + N_EXPERTS * (MOE_TM - 1)) // MOE_TM + 1
    tile_expert, next_expert, weight_slot, src, n_used, dest = _routing_tables(route, tm=MOE_TM, n_tiles=n_tiles)
    ew1 = w1[0].reshape(N_EXPERTS, d, D_EXPERT)
    ew3 = w3[0].reshape(N_EXPERTS, d, D_EXPERT)
    ew2 = w2[0].reshape(N_EXPERTS, D_EXPERT, d)
    y_sorted = _moe_ffn(tile_expert, next_expert, weight_slot, src, n_used, hn_all, ew1, ew3, ew2,
                        n_tiles=n_tiles)
    y_p, y_s = _combine(dest, y_sorted, h_all, route, tp=tp, ts=ts)

    def last_rows(conv_in, n, length):
        return jnp.stack([conv_in[(b + 1) * length - (CONV_W - 1):(b + 1) * length]
                          for b in range(n)]).astype(F32)
    conv_p = last_rows(conv_in_p, bp, lp)
    conv_s = last_rows(conv_in_s, bs, ls)
    return (y_p.reshape(bp, lp, d), y_s.reshape(bs, ls, d),
            conv_p[None], delta_p[None],
            k_p.reshape(1, bp, lp, nh, hd), v_p.reshape(1, bp, lp, nh, hd),
            conv_s[None], delta_s[None],
            k_s.reshape(1, bs, ls, nh, hd), v_s.reshape(1, bs, ls, nh, hd))
```

```python
import functools

import jax
import jax.numpy as jnp
from jax import lax
from jax.experimental import pallas as pl
from jax.experimental.pallas import tpu as pltpu

F32 = jnp.float32
BF16 = jnp.bfloat16

EPS = 1e-6
CHUNK = 64
HEAD_DIM = 128
N_HEADS = 8
D_MODEL = 1024
CONV_W = 4
N_GROUPS = 4
EXPERTS_PER_GROUP = 8
N_EXPERTS = N_GROUPS * EXPERTS_PER_GROUP
D_EXPERT = 512
LANES = 128
SUBLANES = 8
VMEM_LIMIT_BYTES = 56 * 1024 * 1024

PROJ_TM = 512
MERGE_TM = 256
MOE_TM = 256
COMBINE_TM = 256
DELTA_ROWS = 2 * CHUNK
_SB_DEAD = 110.0
_W_CHUNKS = 4
_GATHER_PARTS = 8


def _dot(a, b):
    return jnp.dot(a, b, preferred_element_type=F32)


def _dot_nt(a, b):
    return lax.dot_general(a, b, (((1,), (1,)), ((), ())), preferred_element_type=F32)


def _split_bf16(x):
    hi = x.astype(BF16)
    lo = (x - hi.astype(F32)).astype(BF16)
    return hi, lo


def _sigmoid(x):
    return 1.0 / (1.0 + jnp.exp(-x))


def _softplus(x):
    return jnp.maximum(x, 0.0) + jnp.log(1.0 + jnp.exp(-jnp.abs(x)))


def _rms_rows(x, g):
    ms = jnp.mean(x * x, axis=-1, keepdims=True)
    return x * lax.rsqrt(ms + EPS) * g


assert D_MODEL == SUBLANES * LANES


def _token_rows(t):
    start = t * SUBLANES
    return pl.ds(start if isinstance(start, int) else pl.multiple_of(start, SUBLANES), SUBLANES)


def _to_token_tiles(ref, val):
    for s in range(SUBLANES):
        ref[pl.ds(s, val.shape[0], stride=SUBLANES), :] = val[:, s * LANES:(s + 1) * LANES]


def _token_tile_chunk(ref, n, s):
    return ref[pl.ds(s, n, stride=SUBLANES), :]


def _from_token_tiles(ref, n):
    return jnp.concatenate([_token_tile_chunk(ref, n, s) for s in range(SUBLANES)], axis=1)


def _proj_kernel(x_ref, g1_ref, w_ref, wba_ref, qg_ref, kg_ref, alog_ref, dtb_ref,
                 conv_ref, z_ref, qb_ref, k_ref, v_ref, gate_ref, ba_ref):
    d = D_MODEL
    xn = _rms_rows(x_ref[...], g1_ref[...]).astype(BF16)

    ba = _dot(xn, wba_ref[...])
    lane = lax.broadcasted_iota(jnp.int32, ba.shape, 1)
    g = -jnp.exp(alog_ref[...]) * _softplus(ba + dtb_ref[...])
    ba_ref[...] = jnp.where(lane < N_HEADS, _sigmoid(ba), g)

    def seg(c):
        return _dot(xn, w_ref[:, c * d:(c + 1) * d])

    def head_rms_store(out_ref, val, g):
        for h in range(N_HEADS):
            sl = slice(h * HEAD_DIM, (h + 1) * HEAD_DIM)
            out_ref[:, sl] = _rms_rows(val[:, sl], g).astype(out_ref.dtype)

    for c in range(3):
        conv_ref[:, c * d:(c + 1) * d] = seg(c).astype(conv_ref.dtype)
    z_ref[...] = seg(3).astype(z_ref.dtype)
    head_rms_store(qb_ref, seg(4), qg_ref[...])
    head_rms_store(k_ref, seg(5), kg_ref[...])
    v_ref[...] = seg(6)
    for c in range(2):
        gate_ref[:, c * d:(c + 1) * d] = _sigmoid(seg(7 + c)).astype(gate_ref.dtype)


def _proj(x, g1, w_main, w_ba, qg, kg, alog_pad, dtb_pad):
    t, d = x.shape
    tm = PROJ_TM
    const = lambda i: (0, 0)
    rows = lambda i: (i, 0)

    def resident(shape):
        return pl.BlockSpec(shape, const, pipeline_mode=pl.Buffered(1))

    in_specs = [
        pl.BlockSpec((tm, d), rows),
        resident((1, d)),
        resident(w_main.shape),
        resident((d, LANES)),
        resident((1, HEAD_DIM)),
        resident((1, HEAD_DIM)),
        resident((1, LANES)),
        resident((1, LANES)),
    ]
    widths_dtypes = [(3 * d, BF16),
                     (d, BF16),
                     (d, BF16),
                     (d, F32),
                     (d, F32),
                     (2 * d, BF16),
                     (LANES, F32)]
    return pl.pallas_call(
        _proj_kernel,
        grid=(t // tm,),
        in_specs=in_specs,
        out_specs=[pl.BlockSpec((tm, w), rows) for w, _ in widths_dtypes],
        out_shape=[jax.ShapeDtypeStruct((t, w), dt) for w, dt in widths_dtypes],
        compiler_params=pltpu.CompilerParams(
            dimension_semantics=("arbitrary",),
            vmem_limit_bytes=VMEM_LIMIT_BYTES),
    )(x, g1, w_main, w_ba, qg, kg, alog_pad, dtb_pad)


def _conv_silu(cur, prev8, w):
    acc = cur * w[CONV_W - 1:CONV_W, :]
    rows8 = lax.broadcasted_iota(jnp.int32, prev8.shape, 0)
    for k in range(1, CONV_W):
        sh = pltpu.roll(cur, k, axis=0)
        ph = pltpu.roll(prev8, k, axis=0)
        head = jnp.where(rows8 < k, ph, sh[0:SUBLANES])
        shifted = jnp.concatenate([head, sh[SUBLANES:]], axis=0)
        acc = acc + shifted * w[CONV_W - 1 - k:CONV_W - k, :]
    return acc * _sigmoid(acc)


def _l2_rows(x):
    return x * lax.rsqrt(jnp.sum(x * x, axis=-1, keepdims=True) + EPS)


def _delta_kernel(conv_ref, hist_ref, cw_ref, ba_ref, z_ref, s0_ref, ng_ref, o_ref, sout_ref,
                  s_ref, tail_ref, *, valid_chunks):
    tt = pl.program_id(1)
    R = DELTA_ROWS
    nh, hd = N_HEADS, HEAD_DIM
    heads = range(nh)

    @pl.when(tt == 0)
    def _():
        s_ref[...] = s0_ref[0]
        tail_ref[...] = hist_ref[0]

    raw = conv_ref[...].astype(F32)
    act = _conv_silu(raw, tail_ref[...], cw_ref[...])
    tail_ref[...] = raw[R - SUBLANES:, :]
    Q = [_l2_rows(act[:, h * hd:(h + 1) * hd]) * (hd ** -0.5) for h in heads]
    K = [_l2_rows(act[:, (nh + h) * hd:(nh + h + 1) * hd]) for h in heads]
    V = [act[:, (2 * nh + h) * hd:(2 * nh + h + 1) * hd] for h in heads]

    row = lax.broadcasted_iota(jnp.int32, (R, R), 0)
    col = lax.broadcasted_iota(jnp.int32, (R, R), 1)
    log2 = lambda n: n.bit_length() - 1
    same_chunk = (row >> log2(CHUNK)) == (col >> log2(CHUNK))
    incl = jnp.logical_and(same_chunk, col <= row)
    strict = jnp.logical_and(same_chunk, col < row)
    cs_mat = jnp.where(incl, 1.0, 0.0).astype(BF16)
    eye = jnp.where(row == col, 1.0, 0.0)

    def level_mask(s):
        same = (row >> log2(2 * s)) == (col >> log2(2 * s))
        return jnp.logical_and(jnp.logical_and(same, ((row >> log2(s)) & 1) == 1), ((col >> log2(s)) & 1) == 0)

    ba = ba_ref[...]
    ba_hi, ba_lo = _split_bf16(ba)
    G_all = _dot(cs_mat, ba_hi) + _dot(cs_mat, ba_lo)
    G_all_t = G_all.T

    beta_b = [jnp.broadcast_to(ba[:, h:h + 1], (R, R)) for h in heads]
    Gb = [jnp.broadcast_to(G_all[:, nh + h:nh + h + 1], (R, R)) for h in heads]
    decay = []
    for h in heads:
        g_row = jnp.broadcast_to(G_all_t[nh + h:nh + h + 1, :], (R, R))
        decay.append(jnp.where(incl, jnp.exp(jnp.where(incl, Gb[h] - g_row, 0.0)), 0.0))
    Kb = [K[h].astype(BF16) for h in heads]
    kq = [_dot_nt(jnp.concatenate([Kb[h], Q[h].astype(BF16)], axis=0), Kb[h]) for h in heads]
    Lm = [jnp.where(strict, beta_b[h] * kq[h][:R] * decay[h], 0.0) for h in heads]
    a_qk = [(kq[h][R:] * decay[h]).astype(BF16) for h in heads]

    T = [eye - jnp.where(level_mask(1), Lm[h], 0.0) for h in heads]
    s = 2
    while s < CHUNK:
        mask = level_mask(s)
        Tb = [T[h].astype(BF16) for h in heads]
        X = [_dot(jnp.where(mask, Lm[h], 0.0).astype(BF16), Tb[h]).astype(BF16) for h in heads]
        T = [T[h] - _dot(Tb[h], X[h]) for h in heads]
        s *= 2

    expG = [jnp.exp(Gb[h]) for h in heads]
    uw = [_dot(T[h].astype(BF16),
               jnp.concatenate([beta_b[h] * V[h], beta_b[h] * expG[h] * K[h]], axis=1).astype(BF16))
          for h in heads]
    g_last = [jnp.concatenate([jnp.broadcast_to(Gb[h][CHUNK - 1:CHUNK, :], (CHUNK, R)),
                               jnp.broadcast_to(Gb[h][R - 1:R, :], (CHUNK, R))], axis=0) for h in heads]
    k_tail_t = [(K[h] * jnp.exp(g_last[h] - Gb[h])).T.astype(BF16) for h in heads]

    zeros_half = jnp.zeros((CHUNK, hd), F32)
    S = [s_ref[h] for h in heads]
    o_rows = []
    for c in range(2):
        if c >= valid_chunks:
            o_rows.append(jnp.zeros((CHUNK, nh * hd), F32))
            continue
        rc = slice(c * CHUNK, (c + 1) * CHUNK)
        lhs_s = [jnp.concatenate([uw[h][rc, hd:], Q[h][rc]], axis=0).astype(BF16) for h in heads]
        lhs_u = [jnp.concatenate([a_qk[h][rc], k_tail_t[h]], axis=0) for h in heads]
        ws = [_dot(lhs_s[h], S[h].astype(BF16)) for h in heads]
        U = [uw[h][rc, :hd] - ws[h][:CHUNK] for h in heads]
        Ucat = [(jnp.concatenate([U[h], zeros_half], axis=0) if c == 0
                 else jnp.concatenate([zeros_half, U[h]], axis=0)).astype(BF16) for h in heads]
        au = [_dot(lhs_u[h], Ucat[h]) for h in heads]
        o_rows.append(jnp.concatenate(
            [expG[h][rc] * ws[h][CHUNK:] + au[h][:CHUNK] for h in heads], axis=1))
        S = [jnp.exp(g_last[h][rc][0:1, :]) * S[h] + au[h][CHUNK:] for h in heads]
    for h in heads:
        s_ref[h] = S[h]
    o_raw = jnp.concatenate(o_rows, axis=0)
    z = z_ref[...].astype(F32)
    gate = z * _sigmoid(z)
    for h in heads:
        sl = slice(h * hd, (h + 1) * hd)
        o_ref[:, sl] = _rms_rows(o_raw[:, sl], ng_ref[...]) * gate[:, sl]

    @pl.when(tt == pl.num_programs(1) - 1)
    def _():
        for h in heads:
            sout_ref[0, h] = S[h]


def _delta(conv_in, row_block_off, hist8, conv_w, ba, z, s0, norm_g, *, batch, length, valid_chunks):
    lt = DELTA_ROWS
    nt = length // lt
    nh, hd = N_HEADS, HEAD_DIM
    c_conv = conv_in.shape[1]

    def rows(b, t):
        return (row_block_off + b * nt + t, 0)

    in_specs = [pl.BlockSpec((lt, c_conv), rows),
                pl.BlockSpec((1, SUBLANES, c_conv), lambda b, t: (b, 0, 0)),
                pl.BlockSpec((CONV_W, c_conv), lambda b, t: (0, 0)),
                pl.BlockSpec((lt, LANES), rows),
                pl.BlockSpec((lt, nh * hd), rows),
                pl.BlockSpec((1, nh, hd, hd), lambda b, t: (b, 0, 0, 0)),
                pl.BlockSpec((1, hd), lambda b, t: (0, 0))]
    out_specs = [pl.BlockSpec((lt, nh * hd), lambda b, t: (b * nt + t, 0)),
                 pl.BlockSpec((1, nh, hd, hd), lambda b, t: (b, 0, 0, 0))]
    out_shape = [jax.ShapeDtypeStruct((batch * length, nh * hd), F32),
                 jax.ShapeDtypeStruct((batch, nh, hd, hd), F32)]
    return pl.pallas_call(
        functools.partial(_delta_kernel, valid_chunks=valid_chunks),
        grid=(batch, nt),
        in_specs=in_specs,
        out_specs=out_specs,
        out_shape=out_shape,
        scratch_shapes=[pltpu.VMEM((nh, hd, hd), F32),
                        pltpu.VMEM((SUBLANES, c_conv), F32)],
        compiler_params=pltpu.CompilerParams(
            dimension_semantics=("arbitrary", "arbitrary"),
            vmem_limit_bytes=VMEM_LIMIT_BYTES),
    )(conv_in, hist8, conv_w, ba, z, s0, norm_g)


def _sb_kernel(q_ref, k_ref, v_ref, later2_ref, bias_ref, o_ref, zz_ref, logb_ref, cs_ref, w_ref,
               *, tq, bk, q_start):
    i = pl.program_id(2)
    q = (q_ref[...].astype(F32) * (HEAD_DIM ** -0.5)).astype(BF16)
    q_pos0 = q_start + i * tq
    n_kb = (q_pos0 + tq - 1 + bk - 1) // bk
    later2 = later2_ref[...]

    def key_start(j):
        return pl.multiple_of((n_kb - 1 - jnp.clip(j, 0, n_kb - 1)) * bk, bk)

    def scores(j):
        return _dot_nt(q, k_ref[0, pl.ds(key_start(j), bk), :].astype(BF16))

    def stay_terms(zz):
        s = jnp.maximum(zz, 0.0) + jnp.log(1.0 + jnp.exp(-jnp.abs(zz)))
        hi, lo = _split_bf16(s)
        cs = _dot(jnp.concatenate([hi, lo], axis=1), later2)
        return zz - s, cs, jnp.sum(s, axis=1, keepdims=True)

    def weights(acc):
        return jnp.exp(logb_ref[...] - cs_ref[...] - acc).astype(BF16)

    def weighted_values(j, w):
        return _dot(w, v_ref[0, pl.ds(key_start(j), bk), :].astype(BF16))

    logb_ref[...], cs_ref[...], rs = stay_terms(scores(0) + bias_ref[0])
    zz_ref[...] = scores(1) + bias_ref[1] if tq > bk else scores(1)
    w_ref[...] = jnp.zeros_like(w_ref)
    o_ref[...] = jnp.zeros_like(o_ref)

    def live(acc):
        return (jnp.min(acc) < _SB_DEAD).astype(jnp.int32)

    def cond(carry):
        j, go, _, _ = carry
        return jnp.logical_and(j < n_kb - 1, go > 0)

    def body(carry):
        j, _, acc, rs = carry
        acc_next = acc + rs
        go_next = live(acc_next)
        o_ref[...] += weighted_values(j - 1, w_ref[...])
        w_ref[...] = weights(acc)
        zz_next = zz_ref[...]
        zz_ref[...] = scores(j + 2)
        logb_ref[...], cs_ref[...], rs_next = stay_terms(zz_next)
        return j + 1, go_next, acc_next, rs_next

    j, go, acc, _ = lax.while_loop(cond, body, (jnp.int32(0), jnp.int32(1), jnp.zeros((tq, 1), F32), rs))
    o_ref[...] += weighted_values(j - 1, w_ref[...])

    @pl.when(go > 0)
    def _():
        o_ref[...] += weighted_values(j, weights(acc))


def _sb_attn(q, q_row_block_off, k, v, *, batch, q_len, tq, bk, q_start):
    assert q_start % bk == 0 and (tq in (bk, 2 * bk) or (bk % tq == 0 and q_len == tq))
    nq = q_len // tq
    lk = k.shape[1]
    kv_spec = pl.BlockSpec((1, lk, HEAD_DIM), lambda b, h, i: (b, 0, h))
    ii = jnp.arange(bk, dtype=jnp.int32)
    later = (ii[:, None] > ii[None, :]).astype(BF16)
    later2 = jnp.concatenate([later, later], axis=0)
    n_diag = max(1, tq // bk)
    newest_start = ((tq - 1 + bk - 1) // bk - 1) * bk
    k_rel = (newest_start - bk * jnp.arange(n_diag, dtype=jnp.int32))[:, None, None] + ii[None, None, :]
    q_rel = jnp.arange(tq, dtype=jnp.int32)[None, :, None]
    bias = jnp.where(k_rel < q_rel, 0.0, -1e30).astype(F32)

    def const(shape):
        return pl.BlockSpec(shape, lambda b, h, i: (0,) * len(shape), pipeline_mode=pl.Buffered(1))

    return pl.pallas_call(
        functools.partial(_sb_kernel, tq=tq, bk=bk, q_start=q_start),
        grid=(batch, N_HEADS, nq),
        in_specs=[pl.BlockSpec((tq, HEAD_DIM), lambda b, h, i: (q_row_block_off + b * nq + i, h)),
                  kv_spec, kv_spec, const(later2.shape), const(bias.shape)],
        out_specs=pl.BlockSpec((tq, HEAD_DIM), lambda b, h, i: (b * nq + i, h)),
        out_shape=jax.ShapeDtypeStruct((batch * q_len, N_HEADS * HEAD_DIM), F32),
        scratch_shapes=[pltpu.VMEM((tq, bk), F32)] * 3 + [pltpu.VMEM((tq, bk), BF16)],
        compiler_params=pltpu.CompilerParams(
            dimension_semantics=("arbitrary", "arbitrary", "arbitrary"),
            vmem_limit_bytes=VMEM_LIMIT_BYTES),
    )(q, k, v, later2, bias)


def _merge_kernel(oap_ref, oas_ref, obp_ref, obs_ref, gp_ref, gs_ref, xp_ref, xs_ref,
                  wa_ref, wb_ref, wo_ref, g2_ref, wrh_ref, wrl_ref, br_ref,
                  h_ref, hn_ref, route_ref, *, n_prompt_tiles):
    i = pl.program_id(0)

    def run(oa_ref, ob_ref, gate_ref, x_ref):
        ga = gate_ref[:, :D_MODEL].astype(F32)
        gb = gate_ref[:, D_MODEL:].astype(F32)
        merged = ga * _dot(oa_ref[...].astype(BF16), wa_ref[...]) + gb * _dot(ob_ref[...].astype(BF16), wb_ref[...])
        hh = x_ref[...] + _dot(merged.astype(BF16), wo_ref[...])
        h_ref[...] = hh
        hn = _rms_rows(hh, g2_ref[...])
        _to_token_tiles(hn_ref, hn)
        hi, lo = _split_bf16(hn)
        logits = _dot(hi, wrh_ref[...]) + _dot(lo, wrh_ref[...]) + _dot(hi, wrl_ref[...]) + br_ref[...]
        lane_i = lax.broadcasted_iota(jnp.int32, logits.shape, 1)
        lane = lane_i.astype(F32)
        neg = jnp.float32(-jnp.inf)
        big = jnp.float32(LANES)
        is_g = lane_i < N_GROUPS
        lg = jnp.where(is_g, logits, neg)
        g_max = jnp.max(lg, axis=1, keepdims=True)
        g_idx = jnp.min(jnp.where(lg == g_max, lane, big), axis=1, keepdims=True)
        p_top = 1.0 / jnp.sum(jnp.where(is_g, jnp.exp(lg - g_max), 0.0), axis=1, keepdims=True)
        e_lane = lane_i - N_GROUPS
        lane_group = (e_lane >> (EXPERTS_PER_GROUP.bit_length() - 1)).astype(F32)
        sel = jnp.logical_and(jnp.logical_and(e_lane >= 0, e_lane < N_EXPERTS), lane_group == g_idx)
        le = jnp.where(sel, logits, neg)
        m1 = jnp.max(le, axis=1, keepdims=True)
        i1 = jnp.min(jnp.where(le == m1, lane, big), axis=1, keepdims=True)
        le2 = jnp.where(lane == i1, neg, le)
        m2 = jnp.max(le2, axis=1, keepdims=True)
        i2 = jnp.min(jnp.where(le2 == m2, lane, big), axis=1, keepdims=True)
        e2 = jnp.exp(m2 - m1)
        w1 = p_top / (1.0 + e2)
        w2 = p_top * e2 / (1.0 + e2)
        route = jnp.where(lane_i == 0, i1 - N_GROUPS,
                          jnp.where(lane_i == 1, i2 - N_GROUPS,
                                    jnp.where(lane_i == 2, w1, jnp.where(lane_i == 3, w2, 0.0))))
        route_ref[...] = route

    pl.when(i < n_prompt_tiles)(lambda: run(oap_ref, obp_ref, gp_ref, xp_ref))
    pl.when(i >= n_prompt_tiles)(lambda: run(oas_ref, obs_ref, gs_ref, xs_ref))


def _merge(oa_p, oa_s, ob_p, ob_s, gates_p, gates_s, x_p, x_s, wa, wb, wo, g2, wr_hi, wr_lo, br):
    tp, d = x_p.shape
    ts = x_s.shape[0]
    tm = MERGE_TM
    npt = tp // tm
    n_tiles = npt + ts // tm
    t_all = tp + ts
    p_spec = pl.BlockSpec((tm, d), lambda i: (jnp.minimum(i, npt - 1), 0))
    s_spec = pl.BlockSpec((tm, d), lambda i: (jnp.maximum(i - npt, 0), 0))
    const = lambda i: (0, 0)
    w_spec = pl.BlockSpec((d, d), const)
    in_specs = [p_spec, s_spec, p_spec, s_spec,
                pl.BlockSpec((tm, 2 * d), lambda i: (jnp.minimum(i, npt - 1), 0)),
                pl.BlockSpec((tm, 2 * d), lambda i: (jnp.maximum(i - npt, 0), 0)),
                p_spec, s_spec, w_spec, w_spec, w_spec,
                pl.BlockSpec((1, d), const),
                pl.BlockSpec((d, LANES), const), pl.BlockSpec((d, LANES), const),
                pl.BlockSpec((1, LANES), const)]
    out_specs = [pl.BlockSpec((tm, d), lambda i: (i, 0)),
                 pl.BlockSpec((tm * SUBLANES, LANES), lambda i: (i, 0)),
                 pl.BlockSpec((tm, LANES), lambda i: (i, 0))]
    out_shape = [jax.ShapeDtypeStruct((t_all, d), F32),
                 jax.ShapeDtypeStruct((t_all * SUBLANES, LANES), F32),
                 jax.ShapeDtypeStruct((t_all, LANES), F32)]
    return pl.pallas_call(
        functools.partial(_merge_kernel, n_prompt_tiles=npt),
        grid=(n_tiles,),
        in_specs=in_specs,
        out_specs=out_specs,
        out_shape=out_shape,
        compiler_params=pltpu.CompilerParams(
            dimension_semantics=("arbitrary",),
            vmem_limit_bytes=VMEM_LIMIT_BYTES),
    )(oa_p, oa_s, ob_p, ob_s, gates_p, gates_s, x_p, x_s, wa, wb, wo, g2, wr_hi, wr_lo, br)


def _moe_kernel(te_ref, nxt_ref, wslot_ref, src_ref, nused_ref, hn_hbm, w1_hbm, w3_hbm, w2_hbm, y_ref,
                xbuf, wf1, wf3, wf2, w1b, w3b, w2b, sem, wsem, *, tm):
    i = pl.program_id(0)
    n_used = nused_ref[0]
    slot = i % 2

    def weight_copies(e, ws):
        out = []
        for hbm, buf in ((w1_hbm, wf1), (w3_hbm, wf3), (w2_hbm, wf2)):
            rows = hbm.shape[1] // _W_CHUNKS
            for c in range(_W_CHUNKS):
                part = pl.ds(c * rows, rows)
                out.append(pltpu.make_async_copy(hbm.at[e, part], buf.at[ws, part], wsem.at[ws]))
        return out

    def token_copy(tile, sl, r):
        tok = src_ref[tile * tm + r]
        return pltpu.make_async_copy(hn_hbm.at[_token_rows(tok)], xbuf.at[sl, _token_rows(r)], sem.at[sl])

    def start_gather(tile, sl):
        def issue(r8, c):
            for k in range(SUBLANES):
                token_copy(tile, sl, r8 * SUBLANES + k).start()
            return c
        lax.fori_loop(0, tm // SUBLANES, issue, 0)

    def wait_gather(sl):
        pltpu.make_async_copy(hn_hbm.at[pl.ds(0, tm * SUBLANES)], xbuf.at[sl], sem.at[sl]).wait()

    @pl.when(jnp.logical_and(i == 0, n_used > 0))
    def _():
        start_gather(0, 0)

    nxt_tile = jnp.minimum(i + 1, n_used - 1)

    def prefetch(part):
        per = tm // _GATHER_PARTS
        for k in range(per):
            token_copy(nxt_tile, 1 - slot, part * per + k).start()

    @pl.when(i < n_used)
    def _():
        prev = te_ref[jnp.maximum(i - 1, 0)]
        fresh = jnp.logical_or(i == 0, te_ref[i] != prev)
        ws = wslot_ref[i]

        @pl.when(i == 0)
        def _():
            for cp in weight_copies(te_ref[0], ws):
                cp.start()

        @pl.when(fresh)
        def _():
            for cp in weight_copies(te_ref[i], ws):
                cp.wait()
            w1b[...] = wf1[ws].astype(BF16)
            w3b[...] = wf3[ws].astype(BF16)
            w2b[...] = wf2[ws].astype(BF16)

            @pl.when(nxt_ref[i] >= 0)
            def _():
                for cp in weight_copies(nxt_ref[i], 1 - ws):
                    cp.start()

        wait_gather(slot)
        x = _from_token_tiles(xbuf.at[slot], tm).astype(BF16)
        f = w1b.shape[1]
        part = 0
        hid = []
        for c in range(2):
            cols = slice(c * (f // 2), (c + 1) * (f // 2))
            prefetch(part)
            a = _dot(x, w1b[:, cols])
            prefetch(part + 1)
            hid.append(((a * _sigmoid(a)) * _dot(x, w3b[:, cols])).astype(BF16))
            part += 2
        hid = jnp.concatenate(hid, axis=1)
        n_out = _GATHER_PARTS - part
        chunks = SUBLANES // n_out
        for q in range(n_out):
            prefetch(part + q)
            yq = _dot(hid, w2b[:, q * chunks * LANES:(q + 1) * chunks * LANES])
            for s in range(chunks):
                y_ref[pl.ds(q * chunks + s, tm, stride=SUBLANES), :] = yq[:, s * LANES:(s + 1) * LANES]

        @pl.when(i == n_used - 1)
        def _():
            wait_gather(1 - slot)

    @pl.when(i >= n_used)
    def _():
        y_ref[...] = jnp.zeros_like(y_ref)


def _moe_ffn(tile_expert, next_expert, weight_slot, src, n_used, hn_tiles, w1, w3, w2, *, n_tiles):
    tm = MOE_TM
    d = w1.shape[1]
    f = w1.shape[2]
    any_spec = pl.BlockSpec(memory_space=pl.ANY)
    grid_spec = pltpu.PrefetchScalarGridSpec(
        num_scalar_prefetch=5,
        grid=(n_tiles,),
        in_specs=[any_spec, any_spec, any_spec, any_spec],
        out_specs=pl.BlockSpec((tm * SUBLANES, LANES), lambda i, *_: (i, 0)),
        scratch_shapes=[pltpu.VMEM((2, tm * SUBLANES, LANES), F32),
                        pltpu.VMEM((2, d, f), F32), pltpu.VMEM((2, d, f), F32), pltpu.VMEM((2, f, d), F32),
                        pltpu.VMEM((d, f), BF16), pltpu.VMEM((d, f), BF16), pltpu.VMEM((f, d), BF16),
                        pltpu.SemaphoreType.DMA((2,)), pltpu.SemaphoreType.DMA((2,))])
    return pl.pallas_call(
        functools.partial(_moe_kernel, tm=tm),
        grid_spec=grid_spec,
        out_shape=jax.ShapeDtypeStruct((n_tiles * tm * SUBLANES, LANES), F32),
        compiler_params=pltpu.CompilerParams(
            dimension_semantics=("arbitrary",),
            vmem_limit_bytes=VMEM_LIMIT_BYTES),
    )(tile_expert, next_expert, weight_slot, src, n_used, hn_tiles, w1, w3, w2)


def _combine_kernel(dest_ref, ys_hbm, h_ref, route_ref, yp_ref, ys_ref, buf, sem, *, tm, t_all, n_prompt_tiles):
    i = pl.program_id(0)
    n = pl.num_programs(0)
    slot = i % 2

    def start_copies(tile, sl, r):
        for k in range(2):
            row = dest_ref[k * t_all + tile * tm + r]
            pltpu.make_async_copy(ys_hbm.at[_token_rows(row)], buf.at[sl, k, _token_rows(r)], sem.at[sl]).start()

    def wait_gather(sl):
        for k in range(2):
            pltpu.make_async_copy(ys_hbm.at[pl.ds(0, tm * SUBLANES)], buf.at[sl, k], sem.at[sl]).wait()

    @pl.when(i == 0)
    def _():
        def issue(r8, c):
            for j in range(SUBLANES):
                start_copies(0, 0, r8 * SUBLANES + j)
            return c
        lax.fori_loop(0, tm // SUBLANES, issue, 0)

    nxt_tile = jnp.minimum(i + 1, n - 1)
    per = tm // SUBLANES

    wait_gather(slot)
    route = route_ref[...]
    w0 = jnp.broadcast_to(route[:, 2:3], (tm, LANES))
    w1 = jnp.broadcast_to(route[:, 3:4], (tm, LANES))
    chunks = []
    for s in range(SUBLANES):
        for r in range(s * per, (s + 1) * per):
            start_copies(nxt_tile, 1 - slot, r)
        chunks.append(h_ref[:, s * LANES:(s + 1) * LANES] + w0 * _token_tile_chunk(buf.at[slot, 0], tm, s)
                      + w1 * _token_tile_chunk(buf.at[slot, 1], tm, s))
    y = jnp.concatenate(chunks, axis=1)

    def st(ref):
        ref[...] = y
    pl.when(i < n_prompt_tiles)(lambda: st(yp_ref))
    pl.when(i >= n_prompt_tiles)(lambda: st(ys_ref))

    @pl.when(i == n - 1)
    def _():
        wait_gather(1 - slot)


def _combine(dest, y_sorted, h, route, *, tp, ts):
    tm = COMBINE_TM
    d = h.shape[1]
    t_all = tp + ts
    npt = tp // tm
    n_tiles = t_all // tm
    grid_spec = pltpu.PrefetchScalarGridSpec(
        num_scalar_prefetch=1,
        grid=(n_tiles,),
        in_specs=[pl.BlockSpec(memory_space=pl.ANY),
                  pl.BlockSpec((tm, d), lambda i, de: (i, 0)),
                  pl.BlockSpec((tm, LANES), lambda i, de: (i, 0))],
        out_specs=[pl.BlockSpec((tm, d), lambda i, de: (jnp.minimum(i, npt - 1), 0)),
                   pl.BlockSpec((tm, d), lambda i, de: (jnp.maximum(i - npt, 0), 0))],
        scratch_shapes=[pltpu.VMEM((2, 2, tm * SUBLANES, LANES), F32), pltpu.SemaphoreType.DMA((2,))])
    return pl.pallas_call(
        functools.partial(_combine_kernel, tm=tm, t_all=t_all, n_prompt_tiles=npt),
        grid_spec=grid_spec,
        out_shape=[jax.ShapeDtypeStruct((tp, d), F32), jax.ShapeDtypeStruct((ts, d), F32)],
        compiler_params=pltpu.CompilerParams(
            dimension_semantics=("arbitrary",),
            vmem_limit_bytes=VMEM_LIMIT_BYTES),
    )(dest, y_sorted, h, route)


def _routing_tables(route, *, tm, n_tiles):
    t_all = route.shape[0]
    e_flat = jnp.concatenate([route[:, 0], route[:, 1]]).astype(jnp.int32)
    onehot = (e_flat[:, None] == jnp.arange(N_EXPERTS, dtype=jnp.int32)[None, :]).astype(jnp.int32)
    csum = jnp.cumsum(onehot, axis=0)
    rank = jnp.sum(csum * onehot, axis=1) - 1
    counts = csum[-1]
    tiles_e = (counts + tm - 1) // tm
    tile_end = jnp.cumsum(tiles_e)
    tile_start = tile_end - tiles_e
    dest = (tile_start * tm)[e_flat] + rank
    token = jnp.arange(2 * t_all, dtype=jnp.int32) % t_all
    src = jnp.zeros((n_tiles * tm,), jnp.int32).at[dest].set(token, unique_indices=True)
    tile_expert = jnp.minimum(
        jnp.sum((jnp.arange(n_tiles, dtype=jnp.int32)[:, None] >= tile_end[None, :]).astype(jnp.int32), axis=1),
        N_EXPERTS - 1).astype(jnp.int32)
    n_used = tile_end[-1:].astype(jnp.int32)
    experts = jnp.arange(N_EXPERTS, dtype=jnp.int32)
    has_rows = counts > 0
    later_with_rows = jnp.logical_and(experts[None, :] > experts[:, None], has_rows[None, :])
    next_e = jnp.min(jnp.where(later_with_rows, experts[None, :], N_EXPERTS), axis=1)
    next_e = jnp.where(next_e == N_EXPERTS, -1, next_e).astype(jnp.int32)
    ordinal = jnp.cumsum(has_rows.astype(jnp.int32)) - 1
    next_expert = next_e[tile_expert]
    weight_slot = (ordinal[tile_expert] % 2).astype(jnp.int32)
    return tile_expert, next_expert, weight_slot, src, n_used, dest.astype(jnp.int32)


def _pad_lanes(v, n=LANES, offset=0):
    out = jnp.zeros((1, n), F32)
    return out.at[0, offset:offset + v.shape[0]].set(v.astype(F32))


def kernel(x_prompt, x_sample, cache_conv_a, state_delta_a, cache_k_sb, cache_v_sb, norm1_g, w_in, conv_a_w, a_log, dt_bias, a_out_norm_g, sb_q_norm_g, sb_k_norm_g, w_branch_a, w_branch_b, w_out, norm2_g, w_group, b_group, w_expert_router, b_expert_router, w1, w3, w2):
    assert norm1_g.shape[0] == 1, "single-layer model"
    bp, lp, d = x_prompt.shape
    bs, ls, _ = x_sample.shape
    past = cache_k_sb.shape[2]
    tp, ts = bp * lp, bs * ls
    nh, hd = N_HEADS, HEAD_DIM
    a_conv = 3 * nh * hd

    wi = w_in[0]
    c_z = a_conv + nh * hd
    c_ba = c_z + 2 * nh
    w_main = jnp.concatenate([wi[:, :c_z], wi[:, c_ba:]], axis=1).astype(BF16)
    w_ba = jnp.pad(wi[:, c_z:c_ba], ((0, 0), (0, LANES - 2 * nh))).astype(BF16)
    alog_pad = _pad_lanes(a_log[0], offset=nh)
    dtb_pad = _pad_lanes(dt_bias[0], offset=nh)
    x_p = x_prompt.reshape(tp, d)
    x_s = x_sample.reshape(ts, d)

    proj_w = (norm1_g, w_main, w_ba, sb_q_norm_g, sb_k_norm_g, alog_pad, dtb_pad)
    conv_in_p, z_p, q_p, k_p, v_p, gates_p, ba_p = _proj(x_p, *proj_w)
    conv_in_s, z_s, q_s, k_s, v_s, gates_s, ba_s = _proj(x_s, *proj_w)

    conv_w = conv_a_w[0]
    hist_p = jnp.zeros((bp, SUBLANES, a_conv), F32)
    s0_p = jnp.zeros((bp, nh, hd, hd), F32)
    oa_p, delta_p = _delta(conv_in_p, 0, hist_p, conv_w, ba_p, z_p, s0_p, a_out_norm_g,
                           batch=bp, length=lp, valid_chunks=DELTA_ROWS // CHUNK)
    pad_rows = DELTA_ROWS - ls

    def pad_stream(a):
        a = a.reshape(bs, ls, a.shape[-1])
        return jnp.pad(a, ((0, 0), (0, pad_rows), (0, 0))).reshape(bs * DELTA_ROWS, a.shape[-1])

    hist_s = jnp.pad(cache_conv_a[0], ((0, 0), (SUBLANES - (CONV_W - 1), 0), (0, 0)))
    oa_s_pad, delta_s = _delta(pad_stream(conv_in_s), 0, hist_s, conv_w, pad_stream(ba_s), pad_stream(z_s),
                               state_delta_a[0], a_out_norm_g,
                               batch=bs, length=DELTA_ROWS, valid_chunks=ls // CHUNK)
    oa_s = oa_s_pad.reshape(bs, DELTA_ROWS, nh * hd)[:, :ls].reshape(ts, nh * hd)

    bk = 256
    ob_p = _sb_attn(q_p, 0, k_p.reshape(bp, lp, nh * hd), v_p.reshape(bp, lp, nh * hd),
                    batch=bp, q_len=lp, tq=2 * bk, bk=bk, q_start=0)
    lk_s = past + ls
    lk_pad = -(-lk_s // bk) * bk
    k_all = jnp.concatenate([cache_k_sb[0].reshape(bs, past, nh * hd), k_s.reshape(bs, ls, nh * hd)], axis=1)
    v_all = jnp.concatenate([cache_v_sb[0].reshape(bs, past, nh * hd), v_s.reshape(bs, ls, nh * hd)], axis=1)
    k_all = jnp.pad(k_all, ((0, 0), (0, lk_pad - lk_s), (0, 0)))
    v_all = jnp.pad(v_all, ((0, 0), (0, lk_pad - lk_s), (0, 0)))
    ob_s = _sb_attn(q_s, 0, k_all, v_all, batch=bs, q_len=ls, tq=ls, bk=bk, q_start=past)

    w_router = jnp.concatenate(
        [w_group[0], jnp.moveaxis(w_expert_router[0], 0, 1).reshape(d, N_EXPERTS)], axis=1)
    w_router = jnp.pad(w_router, ((0, 0), (0, LANES - N_GROUPS - N_EXPERTS)))
    wr_hi = w_router.astype(BF16)
    wr_lo = (w_router - wr_hi.astype(F32)).astype(BF16)
    b_router = _pad_lanes(jnp.concatenate([b_group[0], b_expert_router[0].reshape(N_EXPERTS)]))
    h_all, hn_all, route = _merge(oa_p, oa_s, ob_p, ob_s, gates_p, gates_s, x_p, x_s,
                                  w_branch_a[0].astype(BF16), w_branch_b[0].astype(BF16),
                                  w_out[0].astype(BF16), norm2_g, wr_hi, wr_lo, b_router)

    t_all = tp + ts
    n_tiles = (2 * t_all + N_EXPERTS * (MOE_TM - 1)) // MOE_TM + 1
    tile_expert, next_expert, weight_slot, src, n_used, dest = _routing_tables(route, tm=MOE_TM, n_tiles=n_tiles)
    ew1 = w1[0].reshape(N_EXPERTS, d, D_EXPERT)
    ew3 = w3[0].reshape(N_EXPERTS, d, D_EXPERT)
    ew2 = w2[0].reshape(N_EXPERTS, D_EXPERT, d)
    y_sorted = _moe_ffn(tile_expert, next_expert, weight_slot, src, n_used, hn_all, ew1, ew3, ew2,
                        n_tiles=n_tiles)
    y_p, y_s = _combine(dest, y_sorted, h_all, route, tp=tp, ts=ts)

    def last_rows(conv_in, n, length):
        return jnp.stack([conv_in[(b + 1) * length - (CONV_W - 1):(b + 1) * length]
                          for b in range(n)]).astype(F32)
    conv_p = last_rows(conv_in_p, bp, lp)
    conv_s = last_rows(conv_in_s, bs, ls)
    return (y_p.reshape(bp, lp, d), y_s.reshape(bs, ls, d),
            conv_p[None], delta_p[None],
            k_p.reshape(1, bp, lp, nh, hd), v_p.reshape(1, bp, lp, nh, hd),
            conv_s[None], delta_s[None],
            k_s.reshape(1, bs, ls, nh, hd), v_s.reshape(1, bs, ls, nh, hd))
```

```python
import functools

import jax
import jax.numpy as jnp
from jax import lax
from jax.experimental import pallas as pl
from jax.experimental.pallas import tpu as pltpu

F32 = jnp.float32
BF16 = jnp.bfloat16

EPS = 1e-6
CHUNK = 64
HEAD_DIM = 128
N_HEADS = 8
D_MODEL = 1024
CONV_W = 4
N_GROUPS = 4
EXPERTS_PER_GROUP = 8
N_EXPERTS = N_GROUPS * EXPERTS_PER_GROUP
D_EXPERT = 512
LANES = 128
SUBLANES = 8
VMEM_LIMIT_BYTES = 56 * 1024 * 1024

PROJ_TM = 512
MERGE_TM = 256
MOE_TM = 512
COMBINE_TM = 256
DELTA_ROWS = 2 * CHUNK
_SB_DEAD = 110.0
_W_CHUNKS = 4
_SB_HEAD_BLOCKS = 3


def _dot(a, b):
    return jnp.dot(a, b, preferred_element_type=F32)


def _dot_nt(a, b):
    return lax.dot_general(a, b, (((1,), (1,)), ((), ())), preferred_element_type=F32)


def _split_bf16(x):
    hi = x.astype(BF16)
    lo = (x - hi.astype(F32)).astype(BF16)
    return hi, lo


def _sigmoid(x):
    return 1.0 / (1.0 + jnp.exp(-x))


def _softplus(x):
    return jnp.maximum(x, 0.0) + jnp.log(1.0 + jnp.exp(-jnp.abs(x)))


def _rms_rows(x, g):
    ms = jnp.mean(x * x, axis=-1, keepdims=True)
    return x * lax.rsqrt(ms + EPS) * g


assert D_MODEL == SUBLANES * LANES


def _token_rows(t):
    return pl.ds(pl.multiple_of(t * SUBLANES, SUBLANES), SUBLANES)


def _to_token_tiles(ref, val):
    for s in range(SUBLANES):
        ref[pl.ds(s, val.shape[0], stride=SUBLANES), :] = val[:, s * LANES:(s + 1) * LANES]


def _token_tile_chunk(ref, n, s):
    return ref[pl.ds(s, n, stride=SUBLANES), :]


def _from_token_tiles(ref, n):
    return jnp.concatenate([_token_tile_chunk(ref, n, s) for s in range(SUBLANES)], axis=1)


def _proj_kernel(x_ref, g1_ref, w_ref, wba_ref, qg_ref, kg_ref, alog_ref, dtb_ref,
                 conv_ref, z_ref, qb_ref, k_ref, v_ref, gate_ref, ba_ref):
    d = D_MODEL
    xn = _rms_rows(x_ref[...], g1_ref[...]).astype(BF16)

    ba = _dot(xn, wba_ref[...])
    lane = lax.broadcasted_iota(jnp.int32, ba.shape, 1)
    g = -jnp.exp(alog_ref[...]) * _softplus(ba + dtb_ref[...])
    ba_ref[...] = jnp.where(lane < N_HEADS, _sigmoid(ba), g)

    def seg(c):
        return _dot(xn, w_ref[:, c * d:(c + 1) * d])

    def head_rms_store(out_ref, val, g):
        for h in range(N_HEADS):
            sl = slice(h * HEAD_DIM, (h + 1) * HEAD_DIM)
            out_ref[:, sl] = _rms_rows(val[:, sl], g).astype(out_ref.dtype)

    for c in range(3):
        conv_ref[:, c * d:(c + 1) * d] = seg(c).astype(conv_ref.dtype)
    z_ref[...] = seg(3).astype(z_ref.dtype)
    head_rms_store(qb_ref, seg(4), qg_ref[...])
    head_rms_store(k_ref, seg(5), kg_ref[...])
    v_ref[...] = seg(6)
    for c in range(2):
        gate_ref[:, c * d:(c + 1) * d] = _sigmoid(seg(7 + c)).astype(gate_ref.dtype)


def _proj(x, g1, w_main, w_ba, qg, kg, alog_pad, dtb_pad):
    t, d = x.shape
    tm = PROJ_TM
    const = lambda i: (0, 0)
    rows = lambda i: (i, 0)

    def resident(shape):
        return pl.BlockSpec(shape, const, pipeline_mode=pl.Buffered(1))

    in_specs = [
        pl.BlockSpec((tm, d), rows),
        resident((1, d)),
        resident(w_main.shape),
        resident((d, LANES)),
        resident((1, HEAD_DIM)),
        resident((1, HEAD_DIM)),
        resident((1, LANES)),
        resident((1, LANES)),
    ]
    widths_dtypes = [(3 * d, BF16),
                     (d, BF16),
                     (d, BF16),
                     (d, F32),
                     (d, F32),
                     (2 * d, BF16),
                     (LANES, F32)]
    return pl.pallas_call(
        _proj_kernel,
        grid=(t // tm,),
        in_specs=in_specs,
        out_specs=[pl.BlockSpec((tm, w), rows) for w, _ in widths_dtypes],
        out_shape=[jax.ShapeDtypeStruct((t, w), dt) for w, dt in widths_dtypes],
        compiler_params=pltpu.CompilerParams(
            dimension_semantics=("arbitrary",),
            vmem_limit_bytes=VMEM_LIMIT_BYTES),
    )(x, g1, w_main, w_ba, qg, kg, alog_pad, dtb_pad)


def _conv_silu(cur, prev8, w):
    acc = cur * w[CONV_W - 1:CONV_W, :]
    rows8 = lax.broadcasted_iota(jnp.int32, prev8.shape, 0)
    for k in range(1, CONV_W):
        sh = pltpu.roll(cur, k, axis=0)
        ph = pltpu.roll(prev8, k, axis=0)
        head = jnp.where(rows8 < k, ph, sh[0:SUBLANES])
        shifted = jnp.concatenate([head, sh[SUBLANES:]], axis=0)
        acc = acc + shifted * w[CONV_W - 1 - k:CONV_W - k, :]
    return acc * _sigmoid(acc)


def _l2_rows(x):
    return x * lax.rsqrt(jnp.sum(x * x, axis=-1, keepdims=True) + EPS)


def _delta_kernel(conv_ref, hist_ref, cw_ref, ba_ref, z_ref, s0_ref, ng_ref, o_ref, sout_ref,
                  s_ref, tail_ref, *, valid_chunks):
    tt = pl.program_id(1)
    R = DELTA_ROWS
    nh, hd = N_HEADS, HEAD_DIM
    heads = range(nh)

    @pl.when(tt == 0)
    def _():
        s_ref[...] = s0_ref[0]
        tail_ref[...] = hist_ref[0]

    raw = conv_ref[...].astype(F32)
    act = _conv_silu(raw, tail_ref[...], cw_ref[...])
    tail_ref[...] = raw[R - SUBLANES:, :]
    Q = [_l2_rows(act[:, h * hd:(h + 1) * hd]) * (hd ** -0.5) for h in heads]
    K = [_l2_rows(act[:, (nh + h) * hd:(nh + h + 1) * hd]) for h in heads]
    V = [act[:, (2 * nh + h) * hd:(2 * nh + h + 1) * hd] for h in heads]

    row = lax.broadcasted_iota(jnp.int32, (R, R), 0)
    col = lax.broadcasted_iota(jnp.int32, (R, R), 1)
    log2 = lambda n: n.bit_length() - 1
    same_chunk = (row >> log2(CHUNK)) == (col >> log2(CHUNK))
    incl = jnp.logical_and(same_chunk, col <= row)
    strict = jnp.logical_and(same_chunk, col < row)
    cs_mat = jnp.where(incl, 1.0, 0.0).astype(BF16)
    eye = jnp.where(row == col, 1.0, 0.0)

    def level_mask(s):
        same = (row >> log2(2 * s)) == (col >> log2(2 * s))
        return jnp.logical_and(jnp.logical_and(same, ((row >> log2(s)) & 1) == 1), ((col >> log2(s)) & 1) == 0)

    ba = ba_ref[...]
    ba_hi, ba_lo = _split_bf16(ba)
    G_all = _dot(cs_mat, ba_hi) + _dot(cs_mat, ba_lo)
    G_all_t = G_all.T

    beta_b = [jnp.broadcast_to(ba[:, h:h + 1], (R, R)) for h in heads]
    Gb = [jnp.broadcast_to(G_all[:, nh + h:nh + h + 1], (R, R)) for h in heads]
    decay = []
    for h in heads:
        g_row = jnp.broadcast_to(G_all_t[nh + h:nh + h + 1, :], (R, R))
        decay.append(jnp.where(incl, jnp.exp(jnp.where(incl, Gb[h] - g_row, 0.0)), 0.0))
    Kb = [K[h].astype(BF16) for h in heads]
    kq = [_dot_nt(jnp.concatenate([Kb[h], Q[h].astype(BF16)], axis=0), Kb[h]) for h in heads]
    Lm = [jnp.where(strict, beta_b[h] * kq[h][:R] * decay[h], 0.0) for h in heads]
    a_qk = [(kq[h][R:] * decay[h]).astype(BF16) for h in heads]

    T = [eye - jnp.where(level_mask(1), Lm[h], 0.0) for h in heads]
    s = 2
    while s < CHUNK:
        mask = level_mask(s)
        Tb = [T[h].astype(BF16) for h in heads]
        X = [_dot(jnp.where(mask, Lm[h], 0.0).astype(BF16), Tb[h]).astype(BF16) for h in heads]
        T = [T[h] - _dot(Tb[h], X[h]) for h in heads]
        s *= 2

    expG = [jnp.exp(Gb[h]) for h in heads]
    uw = [_dot(T[h].astype(BF16),
               jnp.concatenate([beta_b[h] * V[h], beta_b[h] * expG[h] * K[h]], axis=1).astype(BF16))
          for h in heads]
    g_last = [jnp.concatenate([jnp.broadcast_to(Gb[h][CHUNK - 1:CHUNK, :], (CHUNK, R)),
                               jnp.broadcast_to(Gb[h][R - 1:R, :], (CHUNK, R))], axis=0) for h in heads]
    k_tail_t = [(K[h] * jnp.exp(g_last[h] - Gb[h])).T.astype(BF16) for h in heads]

    zeros_half = jnp.zeros((CHUNK, hd), F32)
    S = [s_ref[h] for h in heads]
    o_rows = []
    for c in range(2):
        if c >= valid_chunks:
            o_rows.append(jnp.zeros((CHUNK, nh * hd), F32))
            continue
        rc = slice(c * CHUNK, (c + 1) * CHUNK)
        lhs_s = [jnp.concatenate([uw[h][rc, hd:], Q[h][rc]], axis=0).astype(BF16) for h in heads]
        lhs_u = [jnp.concatenate([a_qk[h][rc], k_tail_t[h]], axis=0) for h in heads]
        ws = [_dot(lhs_s[h], S[h].astype(BF16)) for h in heads]
        U = [uw[h][rc, :hd] - ws[h][:CHUNK] for h in heads]
        Ucat = [(jnp.concatenate([U[h], zeros_half], axis=0) if c == 0
                 else jnp.concatenate([zeros_half, U[h]], axis=0)).astype(BF16) for h in heads]
        au = [_dot(lhs_u[h], Ucat[h]) for h in heads]
        o_rows.append(jnp.concatenate(
            [expG[h][rc] * ws[h][CHUNK:] + au[h][:CHUNK] for h in heads], axis=1))
        S = [jnp.exp(g_last[h][rc][0:1, :]) * S[h] + au[h][CHUNK:] for h in heads]
    for h in heads:
        s_ref[h] = S[h]
    o_raw = jnp.concatenate(o_rows, axis=0)
    z = z_ref[...].astype(F32)
    gate = z * _sigmoid(z)
    for h in heads:
        sl = slice(h * hd, (h + 1) * hd)
        o_ref[:, sl] = _rms_rows(o_raw[:, sl], ng_ref[...]) * gate[:, sl]

    @pl.when(tt == pl.num_programs(1) - 1)
    def _():
        for h in heads:
            sout_ref[0, h] = S[h]


def _delta(conv_in, row_block_off, hist8, conv_w, ba, z, s0, norm_g, *, batch, length, valid_chunks):
    lt = DELTA_ROWS
    nt = length // lt
    nh, hd = N_HEADS, HEAD_DIM
    c_conv = conv_in.shape[1]

    def rows(b, t):
        return (row_block_off + b * nt + t, 0)

    in_specs = [pl.BlockSpec((lt, c_conv), rows),
                pl.BlockSpec((1, SUBLANES, c_conv), lambda b, t: (b, 0, 0)),
                pl.BlockSpec((CONV_W, c_conv), lambda b, t: (0, 0)),
                pl.BlockSpec((lt, LANES), rows),
                pl.BlockSpec((lt, nh * hd), rows),
                pl.BlockSpec((1, nh, hd, hd), lambda b, t: (b, 0, 0, 0)),
                pl.BlockSpec((1, hd), lambda b, t: (0, 0))]
    out_specs = [pl.BlockSpec((lt, nh * hd), lambda b, t: (b * nt + t, 0)),
                 pl.BlockSpec((1, nh, hd, hd), lambda b, t: (b, 0, 0, 0))]
    out_shape = [jax.ShapeDtypeStruct((batch * length, nh * hd), F32),
                 jax.ShapeDtypeStruct((batch, nh, hd, hd), F32)]
    return pl.pallas_call(
        functools.partial(_delta_kernel, valid_chunks=valid_chunks),
        grid=(batch, nt),
        in_specs=in_specs,
        out_specs=out_specs,
        out_shape=out_shape,
        scratch_shapes=[pltpu.VMEM((nh, hd, hd), F32),
                        pltpu.VMEM((SUBLANES, c_conv), F32)],
        compiler_params=pltpu.CompilerParams(
            dimension_semantics=("arbitrary", "arbitrary"),
            vmem_limit_bytes=VMEM_LIMIT_BYTES),
    )(conv_in, hist8, conv_w, ba, z, s0, norm_g)


def _sb_kernel(q_ref, k_ref, v_ref, later2_ref, o_ref, *, tq, bk, q_start):
    i = pl.program_id(2)
    q = (q_ref[...].astype(F32) * (HEAD_DIM ** -0.5)).astype(BF16)
    q_pos0 = q_start + i * tq
    n_kb = (q_pos0 + tq - 1 + bk - 1) // bk
    later2 = later2_ref[...]
    wide = _SB_HEAD_BLOCKS * bk

    def block_weights(zz, acc):
        s = jnp.maximum(zz, 0.0) + jnp.log(1.0 + jnp.exp(-jnp.abs(zz)))
        hi, lo = _split_bf16(s)
        cs = _dot(jnp.concatenate([hi, lo], axis=1), later2)
        w = jnp.exp(zz - s - cs - acc).astype(BF16)
        return w, acc + jnp.sum(s, axis=1, keepdims=True)

    first_blk = jnp.maximum(n_kb - _SB_HEAD_BLOCKS, 0)
    k0 = pl.multiple_of(first_blk * bk, bk)
    zz = _dot_nt(q, k_ref[0, pl.ds(k0, wide), :].astype(BF16))
    k_pos = k0 + lax.broadcasted_iota(jnp.int32, (tq, wide), 1)
    q_pos = q_pos0 + lax.broadcasted_iota(jnp.int32, (tq, wide), 0)
    zz = jnp.where(k_pos < q_pos, zz, -1e30)
    acc = jnp.zeros((tq, 1), F32)
    w = [None] * _SB_HEAD_BLOCKS
    for c in reversed(range(_SB_HEAD_BLOCKS)):
        w[c], acc = block_weights(zz[:, c * bk:(c + 1) * bk], acc)
    o_ref[...] = _dot(jnp.concatenate(w, axis=1), v_ref[0, pl.ds(k0, wide), :].astype(BF16))

    def live(acc):
        return (jnp.min(acc) < _SB_DEAD).astype(jnp.int32)

    def cond(carry):
        blk, go, _ = carry
        return jnp.logical_and(blk >= 0, go > 0)

    def body(carry):
        blk, _, acc = carry
        k1 = pl.multiple_of(blk * bk, bk)
        zz = _dot_nt(q, k_ref[0, pl.ds(k1, bk), :].astype(BF16))
        w, acc = block_weights(zz, acc)
        o_ref[...] += _dot(w, v_ref[0, pl.ds(k1, bk), :].astype(BF16))
        return blk - 1, live(acc), acc

    lax.while_loop(cond, body, (first_blk - 1, live(acc), acc))


def _sb_attn(q, q_row_block_off, k, v, *, batch, q_len, tq, bk, q_start):
    lk = k.shape[1]
    assert q_start % bk == 0 and lk % bk == 0 and lk >= _SB_HEAD_BLOCKS * bk
    assert tq % bk == 0 and tq // bk < _SB_HEAD_BLOCKS or bk % tq == 0 and q_len == tq
    nq = q_len // tq
    kv_spec = pl.BlockSpec((1, lk, HEAD_DIM), lambda b, h, i: (b, 0, h))
    ii = jnp.arange(bk, dtype=jnp.int32)
    later = (ii[:, None] > ii[None, :]).astype(BF16)
    later2 = jnp.concatenate([later, later], axis=0)
    return pl.pallas_call(
        functools.partial(_sb_kernel, tq=tq, bk=bk, q_start=q_start),
        grid=(batch, N_HEADS, nq),
        in_specs=[pl.BlockSpec((tq, HEAD_DIM), lambda b, h, i: (q_row_block_off + b * nq + i, h)),
                  kv_spec, kv_spec,
                  pl.BlockSpec(later2.shape, lambda b, h, i: (0, 0), pipeline_mode=pl.Buffered(1))],
        out_specs=pl.BlockSpec((tq, HEAD_DIM), lambda b, h, i: (b * nq + i, h)),
        out_shape=jax.ShapeDtypeStruct((batch * q_len, N_HEADS * HEAD_DIM), F32),
        compiler_params=pltpu.CompilerParams(
            dimension_semantics=("arbitrary", "arbitrary", "arbitrary"),
            vmem_limit_bytes=VMEM_LIMIT_BYTES),
    )(q, k, v, later2)


def _merge_kernel(oap_ref, oas_ref, obp_ref, obs_ref, gp_ref, gs_ref, xp_ref, xs_ref,
                  wa_ref, wb_ref, wo_ref, g2_ref, wrh_ref, wrl_ref, br_ref,
                  h_ref, hn_ref, route_ref, *, n_prompt_tiles):
    i = pl.program_id(0)

    def run(oa_ref, ob_ref, gate_ref, x_ref):
        ga = gate_ref[:, :D_MODEL].astype(F32)
        gb = gate_ref[:, D_MODEL:].astype(F32)
        merged = ga * _dot(oa_ref[...].astype(BF16), wa_ref[...]) + gb * _dot(ob_ref[...].astype(BF16), wb_ref[...])
        hh = x_ref[...] + _dot(merged.astype(BF16), wo_ref[...])
        h_ref[...] = hh
        hn = _rms_rows(hh, g2_ref[...])
        _to_token_tiles(hn_ref, hn)
        hi, lo = _split_bf16(hn)
        logits = _dot(hi, wrh_ref[...]) + _dot(lo, wrh_ref[...]) + _dot(hi, wrl_ref[...]) + br_ref[...]
        lane_i = lax.broadcasted_iota(jnp.int32, logits.shape, 1)
        lane = lane_i.astype(F32)
        neg = jnp.float32(-jnp.inf)
        big = jnp.float32(LANES)
        is_g = lane_i < N_GROUPS
        lg = jnp.where(is_g, logits, neg)
        g_max = jnp.max(lg, axis=1, keepdims=True)
        g_idx = jnp.min(jnp.where(lg == g_max, lane, big), axis=1, keepdims=True)
        p_top = 1.0 / jnp.sum(jnp.where(is_g, jnp.exp(lg - g_max), 0.0), axis=1, keepdims=True)
        e_lane = lane_i - N_GROUPS
        lane_group = (e_lane >> (EXPERTS_PER_GROUP.bit_length() - 1)).astype(F32)
        sel = jnp.logical_and(jnp.logical_and(e_lane >= 0, e_lane < N_EXPERTS), lane_group == g_idx)
        le = jnp.where(sel, logits, neg)
        m1 = jnp.max(le, axis=1, keepdims=True)
        i1 = jnp.min(jnp.where(le == m1, lane, big), axis=1, keepdims=True)
        le2 = jnp.where(lane == i1, neg, le)
        m2 = jnp.max(le2, axis=1, keepdims=True)
        i2 = jnp.min(jnp.where(le2 == m2, lane, big), axis=1, keepdims=True)
        e2 = jnp.exp(m2 - m1)
        w1 = p_top / (1.0 + e2)
        w2 = p_top * e2 / (1.0 + e2)
        route = jnp.where(lane_i == 0, i1 - N_GROUPS,
                          jnp.where(lane_i == 1, i2 - N_GROUPS,
                                    jnp.where(lane_i == 2, w1, jnp.where(lane_i == 3, w2, 0.0))))
        route_ref[...] = route

    pl.when(i < n_prompt_tiles)(lambda: run(oap_ref, obp_ref, gp_ref, xp_ref))
    pl.when(i >= n_prompt_tiles)(lambda: run(oas_ref, obs_ref, gs_ref, xs_ref))


def _merge(oa_p, oa_s, ob_p, ob_s, gates_p, gates_s, x_p, x_s, wa, wb, wo, g2, wr_hi, wr_lo, br):
    tp, d = x_p.shape
    ts = x_s.shape[0]
    tm = MERGE_TM
    npt = tp // tm
    n_tiles = npt + ts // tm
    t_all = tp + ts
    p_spec = pl.BlockSpec((tm, d), lambda i: (jnp.minimum(i, npt - 1), 0))
    s_spec = pl.BlockSpec((tm, d), lambda i: (jnp.maximum(i - npt, 0), 0))
    const = lambda i: (0, 0)
    w_spec = pl.BlockSpec((d, d), const)
    in_specs = [p_spec, s_spec, p_spec, s_spec,
                pl.BlockSpec((tm, 2 * d), lambda i: (jnp.minimum(i, npt - 1), 0)),
                pl.BlockSpec((tm, 2 * d), lambda i: (jnp.maximum(i - npt, 0), 0)),
                p_spec, s_spec, w_spec, w_spec, w_spec,
                pl.BlockSpec((1, d), const),
                pl.BlockSpec((d, LANES), const), pl.BlockSpec((d, LANES), const),
                pl.BlockSpec((1, LANES), const)]
    out_specs = [pl.BlockSpec((tm, d), lambda i: (i, 0)),
                 pl.BlockSpec((tm * SUBLANES, LANES), lambda i: (i, 0)),
                 pl.BlockSpec((tm, LANES), lambda i: (i, 0))]
    out_shape = [jax.ShapeDtypeStruct((t_all, d), F32),
                 jax.ShapeDtypeStruct((t_all * SUBLANES, LANES), F32),
                 jax.ShapeDtypeStruct((t_all, LANES), F32)]
    return pl.pallas_call(
        functools.partial(_merge_kernel, n_prompt_tiles=npt),
        grid=(n_tiles,),
        in_specs=in_specs,
        out_specs=out_specs,
        out_shape=out_shape,
        compiler_params=pltpu.CompilerParams(
            dimension_semantics=("arbitrary",),
            vmem_limit_bytes=VMEM_LIMIT_BYTES),
    )(oa_p, oa_s, ob_p, ob_s, gates_p, gates_s, x_p, x_s, wa, wb, wo, g2, wr_hi, wr_lo, br)


def _moe_kernel(te_ref, nxt_ref, wslot_ref, src_ref, nused_ref, hn_hbm, w1_hbm, w3_hbm, w2_hbm, y_ref,
                xbuf, wf1, wf3, wf2, w1b, w3b, w2b, sem, wsem, *, tm):
    i = pl.program_id(0)
    n_used = nused_ref[0]
    slot = i % 2

    def weight_copies(e, ws):
        out = []
        for hbm, buf in ((w1_hbm, wf1), (w3_hbm, wf3), (w2_hbm, wf2)):
            rows = hbm.shape[1] // _W_CHUNKS
            for c in range(_W_CHUNKS):
                part = pl.ds(c * rows, rows)
                out.append(pltpu.make_async_copy(hbm.at[e, part], buf.at[ws, part], wsem.at[ws]))
        return out

    def token_copy(tile, sl, r):
        tok = src_ref[tile * tm + r]
        return pltpu.make_async_copy(hn_hbm.at[_token_rows(tok)], xbuf.at[sl, _token_rows(r)], sem.at[sl])

    def start_gather(tile, sl):
        def issue(r8, c):
            for k in range(SUBLANES):
                token_copy(tile, sl, r8 * SUBLANES + k).start()
            return c
        lax.fori_loop(0, tm // SUBLANES, issue, 0)

    def wait_gather(sl):
        pltpu.make_async_copy(hn_hbm.at[pl.ds(0, tm * SUBLANES)], xbuf.at[sl], sem.at[sl]).wait()

    @pl.when(jnp.logical_and(i == 0, n_used > 0))
    def _():
        start_gather(0, 0)

    @pl.when(i + 1 < n_used)
    def _():
        start_gather(i + 1, 1 - slot)

    @pl.when(i < n_used)
    def _():
        prev = te_ref[jnp.maximum(i - 1, 0)]
        fresh = jnp.logical_or(i == 0, te_ref[i] != prev)
        ws = wslot_ref[i]

        @pl.when(i == 0)
        def _():
            for cp in weight_copies(te_ref[0], ws):
                cp.start()

        @pl.when(fresh)
        def _():
            for cp in weight_copies(te_ref[i], ws):
                cp.wait()
            w1b[...] = wf1[ws].astype(BF16)
            w3b[...] = wf3[ws].astype(BF16)
            w2b[...] = wf2[ws].astype(BF16)

            @pl.when(nxt_ref[i] >= 0)
            def _():
                for cp in weight_copies(nxt_ref[i], 1 - ws):
                    cp.start()

        wait_gather(slot)
        x = _from_token_tiles(xbuf.at[slot], tm).astype(BF16)
        a = _dot(x, w1b[...])
        hid = (a * _sigmoid(a)) * _dot(x, w3b[...])
        _to_token_tiles(y_ref, _dot(hid.astype(BF16), w2b[...]))

    @pl.when(i >= n_used)
    def _():
        y_ref[...] = jnp.zeros_like(y_ref)


def _moe_ffn(tile_expert, next_expert, weight_slot, src, n_used, hn_tiles, w1, w3, w2, *, n_tiles):
    tm = MOE_TM
    d = w1.shape[1]
    f = w1.shape[2]
    any_spec = pl.BlockSpec(memory_space=pl.ANY)
    grid_spec = pltpu.PrefetchScalarGridSpec(
        num_scalar_prefetch=5,
        grid=(n_tiles,),
        in_specs=[any_spec, any_spec, any_spec, any_spec],
        out_specs=pl.BlockSpec((tm * SUBLANES, LANES), lambda i, *_: (i, 0)),
        scratch_shapes=[pltpu.VMEM((2, tm * SUBLANES, LANES), F32),
                        pltpu.VMEM((2, d, f), F32), pltpu.VMEM((2, d, f), F32), pltpu.VMEM((2, f, d), F32),
                        pltpu.VMEM((d, f), BF16), pltpu.VMEM((d, f), BF16), pltpu.VMEM((f, d), BF16),
                        pltpu.SemaphoreType.DMA((2,)), pltpu.SemaphoreType.DMA((2,))])
    return pl.pallas_call(
        functools.partial(_moe_kernel, tm=tm),
        grid_spec=grid_spec,
        out_shape=jax.ShapeDtypeStruct((n_tiles * tm * SUBLANES, LANES), F32),
        compiler_params=pltpu.CompilerParams(
            dimension_semantics=("arbitrary",),
            vmem_limit_bytes=VMEM_LIMIT_BYTES),
    )(tile_expert, next_expert, weight_slot, src, n_used, hn_tiles, w1, w3, w2)


def _combine_kernel(dest_ref, ys_hbm, h_ref, route_ref, yp_ref, ys_ref, buf, sem, *, tm, t_all, n_prompt_tiles):
    i = pl.program_id(0)
    n = pl.num_programs(0)
    slot = i % 2

    def start_gather(tile, sl):
        def issue(r8, c):
            for j in range(SUBLANES):
                r = r8 * SUBLANES + j
                for k in range(2):
                    row = dest_ref[k * t_all + tile * tm + r]
                    pltpu.make_async_copy(ys_hbm.at[_token_rows(row)], buf.at[sl, k, _token_rows(r)],
                                          sem.at[sl]).start()
            return c
        lax.fori_loop(0, tm // SUBLANES, issue, 0)

    def wait_gather(sl):
        for k in range(2):
            pltpu.make_async_copy(ys_hbm.at[pl.ds(0, tm * SUBLANES)], buf.at[sl, k], sem.at[sl]).wait()

    @pl.when(i == 0)
    def _():
        start_gather(0, 0)

    @pl.when(i + 1 < n)
    def _():
        start_gather(i + 1, 1 - slot)

    wait_gather(slot)
    route = route_ref[...]
    w0 = jnp.broadcast_to(route[:, 2:3], (tm, LANES))
    w1 = jnp.broadcast_to(route[:, 3:4], (tm, LANES))

    def emit(out_ref):
        for s in range(SUBLANES):
            cols = slice(s * LANES, (s + 1) * LANES)
            out_ref[:, cols] = (h_ref[:, cols] + w0 * _token_tile_chunk(buf.at[slot, 0], tm, s)
                                + w1 * _token_tile_chunk(buf.at[slot, 1], tm, s))
    pl.when(i < n_prompt_tiles)(lambda: emit(yp_ref))
    pl.when(i >= n_prompt_tiles)(lambda: emit(ys_ref))


def _combine(dest, y_sorted, h, route, *, tp, ts):
    tm = COMBINE_TM
    d = h.shape[1]
    t_all = tp + ts
    npt = tp // tm
    n_tiles = t_all // tm
    grid_spec = pltpu.PrefetchScalarGridSpec(
        num_scalar_prefetch=1,
        grid=(n_tiles,),
        in_specs=[pl.BlockSpec(memory_space=pl.ANY),
                  pl.BlockSpec((tm, d), lambda i, de: (i, 0)),
                  pl.BlockSpec((tm, LANES), lambda i, de: (i, 0))],
        out_specs=[pl.BlockSpec((tm, d), lambda i, de: (jnp.minimum(i, npt - 1), 0)),
                   pl.BlockSpec((tm, d), lambda i, de: (jnp.maximum(i - npt, 0), 0))],
        scratch_shapes=[pltpu.VMEM((2, 2, tm * SUBLANES, LANES), F32), pltpu.SemaphoreType.DMA((2,))])
    return pl.pallas_call(
        functools.partial(_combine_kernel, tm=tm, t_all=t_all, n_prompt_tiles=npt),
        grid_spec=grid_spec,
        out_shape=[jax.ShapeDtypeStruct((tp, d), F32), jax.ShapeDtypeStruct((ts, d), F32)],
        compiler_params=pltpu.CompilerParams(
            dimension_semantics=("arbitrary",),
            vmem_limit_bytes=VMEM_LIMIT_BYTES),
    )(dest, y_sorted, h, route)


def _routing_tables(route, *, tm, n_tiles):
    t_all = route.shape[0]
    e_flat = jnp.concatenate([route[:, 0], route[:, 1]]).astype(jnp.int32)
    onehot = (e_flat[:, None] == jnp.arange(N_EXPERTS, dtype=jnp.int32)[None, :]).astype(jnp.int32)
    csum = jnp.cumsum(onehot, axis=0)
    rank = jnp.sum(csum * onehot, axis=1) - 1
    counts = csum[-1]
    tiles_e = (counts + tm - 1) // tm
    tile_end = jnp.cumsum(tiles_e)
    tile_start = tile_end - tiles_e
    dest = (tile_start * tm)[e_flat] + rank
    token = jnp.arange(2 * t_all, dtype=jnp.int32) % t_all
    src = jnp.zeros((n_tiles * tm,), jnp.int32).at[dest].set(token, unique_indices=True)
    tile_expert = jnp.minimum(
        jnp.sum((jnp.arange(n_tiles, dtype=jnp.int32)[:, None] >= tile_end[None, :]).astype(jnp.int32), axis=1),
        N_EXPERTS - 1).astype(jnp.int32)
    n_used = tile_end[-1:].astype(jnp.int32)
    experts = jnp.arange(N_EXPERTS, dtype=jnp.int32)
    has_rows = counts > 0
    later_with_rows = jnp.logical_and(experts[None, :] > experts[:, None], has_rows[None, :])
    next_e = jnp.min(jnp.where(later_with_rows, experts[None, :], N_EXPERTS), axis=1)
    next_e = jnp.where(next_e == N_EXPERTS, -1, next_e).astype(jnp.int32)
    ordinal = jnp.cumsum(has_rows.astype(jnp.int32)) - 1
    next_expert = next_e[tile_expert]
    weight_slot = (ordinal[tile_expert] % 2).astype(jnp.int32)
    return tile_expert, next_expert, weight_slot, src, n_used, dest.astype(jnp.int32)


def _pad_lanes(v, n=LANES, offset=0):
    out = jnp.zeros((1, n), F32)
    return out.at[0, offset:offset + v.shape[0]].set(v.astype(F32))


def kernel(x_prompt, x_sample, cache_conv_a, state_delta_a, cache_k_sb, cache_v_sb, norm1_g, w_in, conv_a_w, a_log, dt_bias, a_out_norm_g, sb_q_norm_g, sb_k_norm_g, w_branch_a, w_branch_b, w_out, norm2_g, w_group, b_group, w_expert_router, b_expert_router, w1, w3, w2):
    assert norm1_g.shape[0] == 1, "single-layer model"
    bp, lp, d = x_prompt.shape
    bs, ls, _ = x_sample.shape
    past = cache_k_sb.shape[2]
    tp, ts = bp * lp, bs * ls
    nh, hd = N_HEADS, HEAD_DIM
    a_conv = 3 * nh * hd

    wi = w_in[0]
    c_z = a_conv + nh * hd
    c_ba = c_z + 2 * nh
    w_main = jnp.concatenate([wi[:, :c_z], wi[:, c_ba:]], axis=1).astype(BF16)
    w_ba = jnp.pad(wi[:, c_z:c_ba], ((0, 0), (0, LANES - 2 * nh))).astype(BF16)
    alog_pad = _pad_lanes(a_log[0], offset=nh)
    dtb_pad = _pad_lanes(dt_bias[0], offset=nh)
    x_p = x_prompt.reshape(tp, d)
    x_s = x_sample.reshape(ts, d)

    proj_w = (norm1_g, w_main, w_ba, sb_q_norm_g, sb_k_norm_g, alog_pad, dtb_pad)
    conv_in_p, z_p, q_p, k_p, v_p, gates_p, ba_p = _proj(x_p, *proj_w)
    conv_in_s, z_s, q_s, k_s, v_s, gates_s, ba_s = _proj(x_s, *proj_w)

    conv_w = conv_a_w[0]
    hist_p = jnp.zeros((bp, SUBLANES, a_conv), F32)
    s0_p = jnp.zeros((bp, nh, hd, hd), F32)
    oa_p, delta_p = _delta(conv_in_p, 0, hist_p, conv_w, ba_p, z_p, s0_p, a_out_norm_g,
                           batch=bp, length=lp, valid_chunks=DELTA_ROWS // CHUNK)
    pad_rows = DELTA_ROWS - ls

    def pad_stream(a):
        a = a.reshape(bs, ls, a.shape[-1])
        return jnp.pad(a, ((0, 0), (0, pad_rows), (0, 0))).reshape(bs * DELTA_ROWS, a.shape[-1])

    hist_s = jnp.pad(cache_conv_a[0], ((0, 0), (SUBLANES - (CONV_W - 1), 0), (0, 0)))
    oa_s_pad, delta_s = _delta(pad_stream(conv_in_s), 0, hist_s, conv_w, pad_stream(ba_s), pad_stream(z_s),
                               state_delta_a[0], a_out_norm_g,
                               batch=bs, length=DELTA_ROWS, valid_chunks=ls // CHUNK)
    oa_s = oa_s_pad.reshape(bs, DELTA_ROWS, nh * hd)[:, :ls].reshape(ts, nh * hd)

    bk = 256
    ob_p = _sb_attn(q_p, 0, k_p.reshape(bp, lp, nh * hd), v_p.reshape(bp, lp, nh * hd),
                    batch=bp, q_len=lp, tq=2 * bk, bk=bk, q_start=0)
    lk_s = past + ls
    lk_pad = -(-lk_s // bk) * bk
    k_all = jnp.concatenate([cache_k_sb[0].reshape(bs, past, nh * hd), k_s.reshape(bs, ls, nh * hd)], axis=1)
    v_all = jnp.concatenate([cache_v_sb[0].reshape(bs, past, nh * hd), v_s.reshape(bs, ls, nh * hd)], axis=1)
    k_all = jnp.pad(k_all, ((0, 0), (0, lk_pad - lk_s), (0, 0)))
    v_all = jnp.pad(v_all, ((0, 0), (0, lk_pad - lk_s), (0, 0)))
    ob_s = _sb_attn(q_s, 0, k_all, v_all, batch=bs, q_len=ls, tq=ls, bk=bk, q_start=past)

    w_router = jnp.concatenate(
        [w_group[0], jnp.moveaxis(w_expert_router[0], 0, 1).reshape(d, N_EXPERTS)], axis=1)
    w_router = jnp.pad(w_router, ((0, 0), (0, LANES - N_GROUPS - N_EXPERTS)))
    wr_hi = w_router.astype(BF16)
    wr_lo = (w_router - wr_hi.astype(F32)).astype(BF16)
    b_router = _pad_lanes(jnp.concatenate([b_group[0], b_expert_router[0].reshape(N_EXPERTS)]))
    h_all, hn_all, route = _merge(oa_p, oa_s, ob_p, ob_s, gates_p, gates_s, x_p, x_s,
                                  w_branch_a[0].astype(BF16), w_branch_b[0].astype(BF16),
                                  w_out[0].astype(BF16), norm2_g, wr_hi, wr_lo, b_router)

    t_all = tp + ts
    n_tiles = (2 * t_all + N_EXPERTS * (MOE_TM - 1)) // MOE_TM + 1
    tile_expert, next_expert, weight_slot, src, n_used, dest = _routing_tables(route, tm=MOE_TM, n_tiles=n_tiles)
    ew1 = w1[0].reshape(N_EXPERTS, d, D_EXPERT)
    ew3 = w3[0].reshape(N_EXPERTS, d, D_EXPERT)
    ew2 = w2[0].reshape(N_EXPERTS, D_EXPERT, d)
    y_sorted = _moe_ffn(tile_expert, next_expert, weight_slot, src, n_used, hn_all, ew1, ew3, ew2,
                        n_tiles=n_tiles)
    y_p, y_s = _combine(dest, y_sorted, h_all, route, tp=tp, ts=ts)

    def last_rows(conv_in, n, length):
        return jnp.stack([conv_in[(b + 1) * length - (CONV_W - 1):(b + 1) * length]
                          for b in range(n)]).astype(F32)
    conv_p = last_rows(conv_in_p, bp, lp)
    conv_s = last_rows(conv_in_s, bs, ls)
    return (y_p.reshape(bp, lp, d), y_s.reshape(bs, ls, d),
            conv_p[None], delta_p[None],
            k_p.reshape(1, bp, lp, nh, hd), v_p.reshape(1, bp, lp, nh, hd),
            conv_s[None], delta_s[None],
            k_s.reshape(1, bs, ls, nh, hd), v_s.reshape(1, bs, ls, nh, hd))
```

```python
import functools

import jax
import jax.numpy as jnp
from jax import lax
from jax.experimental import pallas as pl
from jax.experimental.pallas import tpu as pltpu

F32 = jnp.float32
BF16 = jnp.bfloat16

EPS = 1e-6
CHUNK = 64
HEAD_DIM = 128
N_HEADS = 8
D_MODEL = 1024
CONV_W = 4
N_GROUPS = 4
EXPERTS_PER_GROUP = 8
N_EXPERTS = N_GROUPS * EXPERTS_PER_GROUP
D_EXPERT = 512
LANES = 128
SUBLANES = 8
VMEM_LIMIT_BYTES = 56 * 1024 * 1024

PROJ_TM = 512
MERGE_TM = 256
MOE_TM = 256
COMBINE_TM = 256
DELTA_ROWS = 2 * CHUNK
_SB_DEAD = 110.0
_W_CHUNKS = 4
_SB_HEAD_BLOCKS = 3


def _dot(a, b):
    return jnp.dot(a, b, preferred_element_type=F32)


def _dot_nt(a, b):
    return lax.dot_general(a, b, (((1,), (1,)), ((), ())), preferred_element_type=F32)


def _split_bf16(x):
    hi = x.astype(BF16)
    lo = (x - hi.astype(F32)).astype(BF16)
    return hi, lo


def _sigmoid(x):
    return 1.0 / (1.0 + jnp.exp(-x))


def _softplus(x):
    return jnp.maximum(x, 0.0) + jnp.log(1.0 + jnp.exp(-jnp.abs(x)))


def _rms_rows(x, g):
    ms = jnp.mean(x * x, axis=-1, keepdims=True)
    return x * lax.rsqrt(ms + EPS) * g


assert D_MODEL == SUBLANES * LANES


def _token_rows(t):
    return pl.ds(pl.multiple_of(t * SUBLANES, SUBLANES), SUBLANES)


def _to_token_tiles(ref, val):
    for s in range(SUBLANES):
        ref[pl.ds(s, val.shape[0], stride=SUBLANES), :] = val[:, s * LANES:(s + 1) * LANES]


def _token_tile_chunk(ref, n, s):
    return ref[pl.ds(s, n, stride=SUBLANES), :]


def _from_token_tiles(ref, n):
    return jnp.concatenate([_token_tile_chunk(ref, n, s) for s in range(SUBLANES)], axis=1)


def _proj_kernel(x_ref, g1_ref, w_ref, wba_ref, qg_ref, kg_ref, alog_ref, dtb_ref,
                 conv_ref, z_ref, qb_ref, k_ref, v_ref, gate_ref, ba_ref):
    d = D_MODEL
    xn = _rms_rows(x_ref[...], g1_ref[...]).astype(BF16)

    ba = _dot(xn, wba_ref[...])
    lane = lax.broadcasted_iota(jnp.int32, ba.shape, 1)
    g = -jnp.exp(alog_ref[...]) * _softplus(ba + dtb_ref[...])
    ba_ref[...] = jnp.where(lane < N_HEADS, _sigmoid(ba), g)

    def seg(c):
        return _dot(xn, w_ref[:, c * d:(c + 1) * d])

    def head_rms_store(out_ref, val, g):
        for h in range(N_HEADS):
            sl = slice(h * HEAD_DIM, (h + 1) * HEAD_DIM)
            out_ref[:, sl] = _rms_rows(val[:, sl], g).astype(out_ref.dtype)

    for c in range(3):
        conv_ref[:, c * d:(c + 1) * d] = seg(c).astype(conv_ref.dtype)
    z_ref[...] = seg(3).astype(z_ref.dtype)
    head_rms_store(qb_ref, seg(4), qg_ref[...])
    head_rms_store(k_ref, seg(5), kg_ref[...])
    v_ref[...] = seg(6)
    for c in range(2):
        gate_ref[:, c * d:(c + 1) * d] = _sigmoid(seg(7 + c)).astype(gate_ref.dtype)


def _proj(x, g1, w_main, w_ba, qg, kg, alog_pad, dtb_pad):
    t, d = x.shape
    tm = PROJ_TM
    const = lambda i: (0, 0)
    rows = lambda i: (i, 0)

    def resident(shape):
        return pl.BlockSpec(shape, const, pipeline_mode=pl.Buffered(1))

    in_specs = [
        pl.BlockSpec((tm, d), rows),
        resident((1, d)),
        resident(w_main.shape),
        resident((d, LANES)),
        resident((1, HEAD_DIM)),
        resident((1, HEAD_DIM)),
        resident((1, LANES)),
        resident((1, LANES)),
    ]
    widths_dtypes = [(3 * d, BF16),
                     (d, BF16),
                     (d, BF16),
                     (d, F32),
                     (d, F32),
                     (2 * d, BF16),
                     (LANES, F32)]
    return pl.pallas_call(
        _proj_kernel,
        grid=(t // tm,),
        in_specs=in_specs,
        out_specs=[pl.BlockSpec((tm, w), rows) for w, _ in widths_dtypes],
        out_shape=[jax.ShapeDtypeStruct((t, w), dt) for w, dt in widths_dtypes],
        compiler_params=pltpu.CompilerParams(
            dimension_semantics=("arbitrary",),
            vmem_limit_bytes=VMEM_LIMIT_BYTES),
    )(x, g1, w_main, w_ba, qg, kg, alog_pad, dtb_pad)


def _conv_silu(cur, prev8, w):
    acc = cur * w[CONV_W - 1:CONV_W, :]
    rows8 = lax.broadcasted_iota(jnp.int32, prev8.shape, 0)
    for k in range(1, CONV_W):
        sh = pltpu.roll(cur, k, axis=0)
        ph = pltpu.roll(prev8, k, axis=0)
        head = jnp.where(rows8 < k, ph, sh[0:SUBLANES])
        shifted = jnp.concatenate([head, sh[SUBLANES:]], axis=0)
        acc = acc + shifted * w[CONV_W - 1 - k:CONV_W - k, :]
    return acc * _sigmoid(acc)


def _l2_rows(x):
    return x * lax.rsqrt(jnp.sum(x * x, axis=-1, keepdims=True) + EPS)


def _delta_kernel(conv_ref, hist_ref, cw_ref, ba_ref, z_ref, s0_ref, ng_ref, o_ref, sout_ref,
                  s_ref, tail_ref, *, valid_chunks):
    tt = pl.program_id(1)
    R = DELTA_ROWS
    nh, hd = N_HEADS, HEAD_DIM
    heads = range(nh)

    @pl.when(tt == 0)
    def _():
        s_ref[...] = s0_ref[0]
        tail_ref[...] = hist_ref[0]

    raw = conv_ref[...].astype(F32)
    act = _conv_silu(raw, tail_ref[...], cw_ref[...])
    tail_ref[...] = raw[R - SUBLANES:, :]
    Q = [_l2_rows(act[:, h * hd:(h + 1) * hd]) * (hd ** -0.5) for h in heads]
    K = [_l2_rows(act[:, (nh + h) * hd:(nh + h + 1) * hd]) for h in heads]
    V = [act[:, (2 * nh + h) * hd:(2 * nh + h + 1) * hd] for h in heads]

    row = lax.broadcasted_iota(jnp.int32, (R, R), 0)
    col = lax.broadcasted_iota(jnp.int32, (R, R), 1)
    log2 = lambda n: n.bit_length() - 1
    same_chunk = (row >> log2(CHUNK)) == (col >> log2(CHUNK))
    incl = jnp.logical_and(same_chunk, col <= row)
    strict = jnp.logical_and(same_chunk, col < row)
    cs_mat = jnp.where(incl, 1.0, 0.0).astype(BF16)
    eye = jnp.where(row == col, 1.0, 0.0)

    def level_mask(s):
        same = (row >> log2(2 * s)) == (col >> log2(2 * s))
        return jnp.logical_and(jnp.logical_and(same, ((row >> log2(s)) & 1) == 1), ((col >> log2(s)) & 1) == 0)

    ba = ba_ref[...]
    ba_hi, ba_lo = _split_bf16(ba)
    G_all = _dot(cs_mat, ba_hi) + _dot(cs_mat, ba_lo)
    G_all_t = G_all.T

    beta_b = [jnp.broadcast_to(ba[:, h:h + 1], (R, R)) for h in heads]
    Gb = [jnp.broadcast_to(G_all[:, nh + h:nh + h + 1], (R, R)) for h in heads]
    decay = []
    for h in heads:
        g_row = jnp.broadcast_to(G_all_t[nh + h:nh + h + 1, :], (R, R))
        decay.append(jnp.where(incl, jnp.exp(jnp.where(incl, Gb[h] - g_row, 0.0)), 0.0))
    Kb = [K[h].astype(BF16) for h in heads]
    kq = [_dot_nt(jnp.concatenate([Kb[h], Q[h].astype(BF16)], axis=0), Kb[h]) for h in heads]
    Lm = [jnp.where(strict, beta_b[h] * kq[h][:R] * decay[h], 0.0) for h in heads]
    a_qk = [(kq[h][R:] * decay[h]).astype(BF16) for h in heads]

    T = [eye - jnp.where(level_mask(1), Lm[h], 0.0) for h in heads]
    s = 2
    while s < CHUNK:
        mask = level_mask(s)
        Tb = [T[h].astype(BF16) for h in heads]
        X = [_dot(jnp.where(mask, Lm[h], 0.0).astype(BF16), Tb[h]).astype(BF16) for h in heads]
        T = [T[h] - _dot(Tb[h], X[h]) for h in heads]
        s *= 2

    expG = [jnp.exp(Gb[h]) for h in heads]
    uw = [_dot(T[h].astype(BF16),
               jnp.concatenate([beta_b[h] * V[h], beta_b[h] * expG[h] * K[h]], axis=1).astype(BF16))
          for h in heads]
    g_last = [jnp.concatenate([jnp.broadcast_to(Gb[h][CHUNK - 1:CHUNK, :], (CHUNK, R)),
                               jnp.broadcast_to(Gb[h][R - 1:R, :], (CHUNK, R))], axis=0) for h in heads]
    k_tail_t = [(K[h] * jnp.exp(g_last[h] - Gb[h])).T.astype(BF16) for h in heads]

    zeros_half = jnp.zeros((CHUNK, hd), F32)
    S = [s_ref[h] for h in heads]
    o_rows = []
    for c in range(2):
        if c >= valid_chunks:
            o_rows.append(jnp.zeros((CHUNK, nh * hd), F32))
            continue
        rc = slice(c * CHUNK, (c + 1) * CHUNK)
        lhs_s = [jnp.concatenate([uw[h][rc, hd:], Q[h][rc]], axis=0).astype(BF16) for h in heads]
        lhs_u = [jnp.concatenate([a_qk[h][rc], k_tail_t[h]], axis=0) for h in heads]
        ws = [_dot(lhs_s[h], S[h].astype(BF16)) for h in heads]
        U = [uw[h][rc, :hd] - ws[h][:CHUNK] for h in heads]
        Ucat = [(jnp.concatenate([U[h], zeros_half], axis=0) if c == 0
                 else jnp.concatenate([zeros_half, U[h]], axis=0)).astype(BF16) for h in heads]
        au = [_dot(lhs_u[h], Ucat[h]) for h in heads]
        o_rows.append(jnp.concatenate(
            [expG[h][rc] * ws[h][CHUNK:] + au[h][:CHUNK] for h in heads], axis=1))
        S = [jnp.exp(g_last[h][rc][0:1, :]) * S[h] + au[h][CHUNK:] for h in heads]
    for h in heads:
        s_ref[h] = S[h]
    o_raw = jnp.concatenate(o_rows, axis=0)
    z = z_ref[...].astype(F32)
    gate = z * _sigmoid(z)
    for h in heads:
        sl = slice(h * hd, (h + 1) * hd)
        o_ref[:, sl] = _rms_rows(o_raw[:, sl], ng_ref[...]) * gate[:, sl]

    @pl.when(tt == pl.num_programs(1) - 1)
    def _():
        for h in heads:
            sout_ref[0, h] = S[h]


def _delta(conv_in, row_block_off, hist8, conv_w, ba, z, s0, norm_g, *, batch, length, valid_chunks):
    lt = DELTA_ROWS
    nt = length // lt
    nh, hd = N_HEADS, HEAD_DIM
    c_conv = conv_in.shape[1]

    def rows(b, t):
        return (row_block_off + b * nt + t, 0)

    in_specs = [pl.BlockSpec((lt, c_conv), rows),
                pl.BlockSpec((1, SUBLANES, c_conv), lambda b, t: (b, 0, 0)),
                pl.BlockSpec((CONV_W, c_conv), lambda b, t: (0, 0)),
                pl.BlockSpec((lt, LANES), rows),
                pl.BlockSpec((lt, nh * hd), rows),
                pl.BlockSpec((1, nh, hd, hd), lambda b, t: (b, 0, 0, 0)),
                pl.BlockSpec((1, hd), lambda b, t: (0, 0))]
    out_specs = [pl.BlockSpec((lt, nh * hd), lambda b, t: (b * nt + t, 0)),
                 pl.BlockSpec((1, nh, hd, hd), lambda b, t: (b, 0, 0, 0))]
    out_shape = [jax.ShapeDtypeStruct((batch * length, nh * hd), F32),
                 jax.ShapeDtypeStruct((batch, nh, hd, hd), F32)]
    return pl.pallas_call(
        functools.partial(_delta_kernel, valid_chunks=valid_chunks),
        grid=(batch, nt),
        in_specs=in_specs,
        out_specs=out_specs,
        out_shape=out_shape,
        scratch_shapes=[pltpu.VMEM((nh, hd, hd), F32),
                        pltpu.VMEM((SUBLANES, c_conv), F32)],
        compiler_params=pltpu.CompilerParams(
            dimension_semantics=("arbitrary", "arbitrary"),
            vmem_limit_bytes=VMEM_LIMIT_BYTES),
    )(conv_in, hist8, conv_w, ba, z, s0, norm_g)


def _sb_kernel(q_ref, k_ref, v_ref, later2_ref, o_ref, *, tq, bk, q_start):
    i = pl.program_id(2)
    q = (q_ref[...].astype(F32) * (HEAD_DIM ** -0.5)).astype(BF16)
    q_pos0 = q_start + i * tq
    n_kb = (q_pos0 + tq - 1 + bk - 1) // bk
    later2 = later2_ref[...]
    wide = _SB_HEAD_BLOCKS * bk

    def block_weights(zz, acc):
        s = jnp.maximum(zz, 0.0) + jnp.log(1.0 + jnp.exp(-jnp.abs(zz)))
        hi, lo = _split_bf16(s)
        cs = _dot(jnp.concatenate([hi, lo], axis=1), later2)
        w = jnp.exp(zz - s - cs - acc).astype(BF16)
        return w, acc + jnp.sum(s, axis=1, keepdims=True)

    first_blk = jnp.maximum(n_kb - _SB_HEAD_BLOCKS, 0)
    k0 = pl.multiple_of(first_blk * bk, bk)
    zz = _dot_nt(q, k_ref[0, pl.ds(k0, wide), :].astype(BF16))
    k_pos = k0 + lax.broadcasted_iota(jnp.int32, (tq, wide), 1)
    q_pos = q_pos0 + lax.broadcasted_iota(jnp.int32, (tq, wide), 0)
    zz = jnp.where(k_pos < q_pos, zz, -1e30)
    acc = jnp.zeros((tq, 1), F32)
    w = [None] * _SB_HEAD_BLOCKS
    for c in reversed(range(_SB_HEAD_BLOCKS)):
        w[c], acc = block_weights(zz[:, c * bk:(c + 1) * bk], acc)
    o_ref[...] = _dot(jnp.concatenate(w, axis=1), v_ref[0, pl.ds(k0, wide), :].astype(BF16))

    def live(acc):
        return (jnp.min(acc) < _SB_DEAD).astype(jnp.int32)

    def cond(carry):
        blk, go, _ = carry
        return jnp.logical_and(blk >= 0, go > 0)

    def body(carry):
        blk, _, acc = carry
        k1 = pl.multiple_of(blk * bk, bk)
        zz = _dot_nt(q, k_ref[0, pl.ds(k1, bk), :].astype(BF16))
        w, acc = block_weights(zz, acc)
        o_ref[...] += _dot(w, v_ref[0, pl.ds(k1, bk), :].astype(BF16))
        return blk - 1, live(acc), acc

    lax.while_loop(cond, body, (first_blk - 1, live(acc), acc))


def _sb_attn(q, q_row_block_off, k, v, *, batch, q_len, tq, bk, q_start):
    lk = k.shape[1]
    assert q_start % bk == 0 and lk % bk == 0 and lk >= _SB_HEAD_BLOCKS * bk
    assert tq % bk == 0 and tq // bk < _SB_HEAD_BLOCKS or bk % tq == 0 and q_len == tq
    nq = q_len // tq
    kv_spec = pl.BlockSpec((1, lk, HEAD_DIM), lambda b, h, i: (b, 0, h))
    ii = jnp.arange(bk, dtype=jnp.int32)
    later = (ii[:, None] > ii[None, :]).astype(BF16)
    later2 = jnp.concatenate([later, later], axis=0)
    return pl.pallas_call(
        functools.partial(_sb_kernel, tq=tq, bk=bk, q_start=q_start),
        grid=(batch, N_HEADS, nq),
        in_specs=[pl.BlockSpec((tq, HEAD_DIM), lambda b, h, i: (q_row_block_off + b * nq + i, h)),
                  kv_spec, kv_spec,
                  pl.BlockSpec(later2.shape, lambda b, h, i: (0, 0), pipeline_mode=pl.Buffered(1))],
        out_specs=pl.BlockSpec((tq, HEAD_DIM), lambda b, h, i: (b * nq + i, h)),
        out_shape=jax.ShapeDtypeStruct((batch * q_len, N_HEADS * HEAD_DIM), F32),
        compiler_params=pltpu.CompilerParams(
            dimension_semantics=("arbitrary", "arbitrary", "arbitrary"),
            vmem_limit_bytes=VMEM_LIMIT_BYTES),
    )(q, k, v, later2)


def _merge_kernel(oap_ref, oas_ref, obp_ref, obs_ref, gp_ref, gs_ref, xp_ref, xs_ref,
                  wa_ref, wb_ref, wo_ref, g2_ref, wrh_ref, wrl_ref, br_ref,
                  h_ref, hn_ref, route_ref, *, n_prompt_tiles):
    i = pl.program_id(0)

    def run(oa_ref, ob_ref, gate_ref, x_ref):
        ga = gate_ref[:, :D_MODEL].astype(F32)
        gb = gate_ref[:, D_MODEL:].astype(F32)
        merged = ga * _dot(oa_ref[...].astype(BF16), wa_ref[...]) + gb * _dot(ob_ref[...].astype(BF16), wb_ref[...])
        hh = x_ref[...] + _dot(merged.astype(BF16), wo_ref[...])
        h_ref[...] = hh
        hn = _rms_rows(hh, g2_ref[...])
        _to_token_tiles(hn_ref, hn)
        hi, lo = _split_bf16(hn)
        logits = _dot(hi, wrh_ref[...]) + _dot(lo, wrh_ref[...]) + _dot(hi, wrl_ref[...]) + br_ref[...]
        lane_i = lax.broadcasted_iota(jnp.int32, logits.shape, 1)
        lane = lane_i.astype(F32)
        neg = jnp.float32(-jnp.inf)
        big = jnp.float32(LANES)
        is_g = lane_i < N_GROUPS
        lg = jnp.where(is_g, logits, neg)
        g_max = jnp.max(lg, axis=1, keepdims=True)
        g_idx = jnp.min(jnp.where(lg == g_max, lane, big), axis=1, keepdims=True)
        p_top = 1.0 / jnp.sum(jnp.where(is_g, jnp.exp(lg - g_max), 0.0), axis=1, keepdims=True)
        e_lane = lane_i - N_GROUPS
        lane_group = (e_lane >> (EXPERTS_PER_GROUP.bit_length() - 1)).astype(F32)
        sel = jnp.logical_and(jnp.logical_and(e_lane >= 0, e_lane < N_EXPERTS), lane_group == g_idx)
        le = jnp.where(sel, logits, neg)
        m1 = jnp.max(le, axis=1, keepdims=True)
        i1 = jnp.min(jnp.where(le == m1, lane, big), axis=1, keepdims=True)
        le2 = jnp.where(lane == i1, neg, le)
        m2 = jnp.max(le2, axis=1, keepdims=True)
        i2 = jnp.min(jnp.where(le2 == m2, lane, big), axis=1, keepdims=True)
        e2 = jnp.exp(m2 - m1)
        w1 = p_top / (1.0 + e2)
        w2 = p_top * e2 / (1.0 + e2)
        route = jnp.where(lane_i == 0, i1 - N_GROUPS,
                          jnp.where(lane_i == 1, i2 - N_GROUPS,
                                    jnp.where(lane_i == 2, w1, jnp.where(lane_i == 3, w2, 0.0))))
        route_ref[...] = route

    pl.when(i < n_prompt_tiles)(lambda: run(oap_ref, obp_ref, gp_ref, xp_ref))
    pl.when(i >= n_prompt_tiles)(lambda: run(oas_ref, obs_ref, gs_ref, xs_ref))


def _merge(oa_p, oa_s, ob_p, ob_s, gates_p, gates_s, x_p, x_s, wa, wb, wo, g2, wr_hi, wr_lo, br):
    tp, d = x_p.shape
    ts = x_s.shape[0]
    tm = MERGE_TM
    npt = tp // tm
    n_tiles = npt + ts // tm
    t_all = tp + ts
    p_spec = pl.BlockSpec((tm, d), lambda i: (jnp.minimum(i, npt - 1), 0))
    s_spec = pl.BlockSpec((tm, d), lambda i: (jnp.maximum(i - npt, 0), 0))
    const = lambda i: (0, 0)
    w_spec = pl.BlockSpec((d, d), const)
    in_specs = [p_spec, s_spec, p_spec, s_spec,
                pl.BlockSpec((tm, 2 * d), lambda i: (jnp.minimum(i, npt - 1), 0)),
                pl.BlockSpec((tm, 2 * d), lambda i: (jnp.maximum(i - npt, 0), 0)),
                p_spec, s_spec, w_spec, w_spec, w_spec,
                pl.BlockSpec((1, d), const),
                pl.BlockSpec((d, LANES), const), pl.BlockSpec((d, LANES), const),
                pl.BlockSpec((1, LANES), const)]
    out_specs = [pl.BlockSpec((tm, d), lambda i: (i, 0)),
                 pl.BlockSpec((tm * SUBLANES, LANES), lambda i: (i, 0)),
                 pl.BlockSpec((tm, LANES), lambda i: (i, 0))]
    out_shape = [jax.ShapeDtypeStruct((t_all, d), F32),
                 jax.ShapeDtypeStruct((t_all * SUBLANES, LANES), F32),
                 jax.ShapeDtypeStruct((t_all, LANES), F32)]
    return pl.pallas_call(
        functools.partial(_merge_kernel, n_prompt_tiles=npt),
        grid=(n_tiles,),
        in_specs=in_specs,
        out_specs=out_specs,
        out_shape=out_shape,
        compiler_params=pltpu.CompilerParams(
            dimension_semantics=("arbitrary",),
            vmem_limit_bytes=VMEM_LIMIT_BYTES),
    )(oa_p, oa_s, ob_p, ob_s, gates_p, gates_s, x_p, x_s, wa, wb, wo, g2, wr_hi, wr_lo, br)


def _moe_kernel(te_ref, nxt_ref, wslot_ref, src_ref, nused_ref, hn_hbm, w1_hbm, w3_hbm, w2_hbm, y_ref,
                xbuf, wf1, wf3, wf2, w1b, w3b, w2b, sem, wsem, *, tm):
    i = pl.program_id(0)
    n_used = nused_ref[0]
    slot = i % 2

    def weight_copies(e, ws):
        out = []
        for hbm, buf in ((w1_hbm, wf1), (w3_hbm, wf3), (w2_hbm, wf2)):
            rows = hbm.shape[1] // _W_CHUNKS
            for c in range(_W_CHUNKS):
                part = pl.ds(c * rows, rows)
                out.append(pltpu.make_async_copy(hbm.at[e, part], buf.at[ws, part], wsem.at[ws]))
        return out

    def token_copy(tile, sl, r):
        tok = src_ref[tile * tm + r]
        return pltpu.make_async_copy(hn_hbm.at[_token_rows(tok)], xbuf.at[sl, _token_rows(r)], sem.at[sl])

    def start_gather(tile, sl):
        def issue(r8, c):
            for k in range(SUBLANES):
                token_copy(tile, sl, r8 * SUBLANES + k).start()
            return c
        lax.fori_loop(0, tm // SUBLANES, issue, 0)

    def wait_gather(sl):
        pltpu.make_async_copy(hn_hbm.at[pl.ds(0, tm * SUBLANES)], xbuf.at[sl], sem.at[sl]).wait()

    @pl.when(jnp.logical_and(i == 0, n_used > 0))
    def _():
        start_gather(0, 0)

    @pl.when(i + 1 < n_used)
    def _():
        start_gather(i + 1, 1 - slot)

    @pl.when(i < n_used)
    def _():
        prev = te_ref[jnp.maximum(i - 1, 0)]
        fresh = jnp.logical_or(i == 0, te_ref[i] != prev)
        ws = wslot_ref[i]

        @pl.when(i == 0)
        def _():
            for cp in weight_copies(te_ref[0], ws):
                cp.start()

        @pl.when(fresh)
        def _():
            for cp in weight_copies(te_ref[i], ws):
                cp.wait()
            w1b[...] = wf1[ws].astype(BF16)
            w3b[...] = wf3[ws].astype(BF16)
            w2b[...] = wf2[ws].astype(BF16)

            @pl.when(nxt_ref[i] >= 0)
            def _():
                for cp in weight_copies(nxt_ref[i], 1 - ws):
                    cp.start()

        wait_gather(slot)
        x = _from_token_tiles(xbuf.at[slot], tm).astype(BF16)
        a = _dot(x, w1b[...])
        hid = (a * _sigmoid(a)) * _dot(x, w3b[...])
        _to_token_tiles(y_ref, _dot(hid.astype(BF16), w2b[...]))

    @pl.when(i >= n_used)
    def _():
        y_ref[...] = jnp.zeros_like(y_ref)


def _moe_ffn(tile_expert, next_expert, weight_slot, src, n_used, hn_tiles, w1, w3, w2, *, n_tiles):
    tm = MOE_TM
    d = w1.shape[1]
    f = w1.shape[2]
    any_spec = pl.BlockSpec(memory_space=pl.ANY)
    grid_spec = pltpu.PrefetchScalarGridSpec(
        num_scalar_prefetch=5,
        grid=(n_tiles,),
        in_specs=[any_spec, any_spec, any_spec, any_spec],
        out_specs=pl.BlockSpec((tm * SUBLANES, LANES), lambda i, *_: (i, 0)),
        scratch_shapes=[pltpu.VMEM((2, tm * SUBLANES, LANES), F32),
                        pltpu.VMEM((2, d, f), F32), pltpu.VMEM((2, d, f), F32), pltpu.VMEM((2, f, d), F32),
                        pltpu.VMEM((d, f), BF16), pltpu.VMEM((d, f), BF16), pltpu.VMEM((f, d), BF16),
                        pltpu.SemaphoreType.DMA((2,)), pltpu.SemaphoreType.DMA((2,))])
    return pl.pallas_call(
        functools.partial(_moe_kernel, tm=tm),
        grid_spec=grid_spec,
        out_shape=jax.ShapeDtypeStruct((n_tiles * tm * SUBLANES, LANES), F32),
        compiler_params=pltpu.CompilerParams(
            dimension_semantics=("arbitrary",),
            vmem_limit_bytes=VMEM_LIMIT_BYTES),
    )(tile_expert, next_expert, weight_slot, src, n_used, hn_tiles, w1, w3, w2)


def _combine_kernel(dest_ref, ys_hbm, h_ref, route_ref, yp_ref, ys_ref, buf, sem, *, tm, t_all, n_prompt_tiles):
    i = pl.program_id(0)
    n = pl.num_programs(0)
    slot = i % 2

    def start_gather(tile, sl):
        def issue(r8, c):
            for j in range(SUBLANES):
                r = r8 * SUBLANES + j
                for k in range(2):
                    row = dest_ref[k * t_all + tile * tm + r]
                    pltpu.make_async_copy(ys_hbm.at[_token_rows(row)], buf.at[sl, k, _token_rows(r)],
                                          sem.at[sl]).start()
            return c
        lax.fori_loop(0, tm // SUBLANES, issue, 0)

    def wait_gather(sl):
        for k in range(2):
            pltpu.make_async_copy(ys_hbm.at[pl.ds(0, tm * SUBLANES)], buf.at[sl, k], sem.at[sl]).wait()

    @pl.when(i == 0)
    def _():
        start_gather(0, 0)

    @pl.when(i + 1 < n)
    def _():
        start_gather(i + 1, 1 - slot)

    wait_gather(slot)
    route = route_ref[...]
    w0 = jnp.broadcast_to(route[:, 2:3], (tm, LANES))
    w1 = jnp.broadcast_to(route[:, 3:4], (tm, LANES))

    def emit(out_ref):
        for s in range(SUBLANES):
            cols = slice(s * LANES, (s + 1) * LANES)
            out_ref[:, cols] = (h_ref[:, cols] + w0 * _token_tile_chunk(buf.at[slot, 0], tm, s)
                                + w1 * _token_tile_chunk(buf.at[slot, 1], tm, s))
    pl.when(i < n_prompt_tiles)(lambda: emit(yp_ref))
    pl.when(i >= n_prompt_tiles)(lambda: emit(ys_ref))


def _combine(dest, y_sorted, h, route, *, tp, ts):
    tm = COMBINE_TM
    d = h.shape[1]
    t_all = tp + ts
    npt = tp // tm
    n_tiles = t_all // tm
    grid_spec = pltpu.PrefetchScalarGridSpec(
        num_scalar_prefetch=1,
        grid=(n_tiles,),
        in_specs=[pl.BlockSpec(memory_space=pl.ANY),
                  pl.BlockSpec((tm, d), lambda i, de: (i, 0)),
                  pl.BlockSpec((tm, LANES), lambda i, de: (i, 0))],
        out_specs=[pl.BlockSpec((tm, d), lambda i, de: (jnp.minimum(i, npt - 1), 0)),
                   pl.BlockSpec((tm, d), lambda i, de: (jnp.maximum(i - npt, 0), 0))],
        scratch_shapes=[pltpu.VMEM((2, 2, tm * SUBLANES, LANES), F32), pltpu.SemaphoreType.DMA((2,))])
    return pl.pallas_call(
        functools.partial(_combine_kernel, tm=tm, t_all=t_all, n_prompt_tiles=npt),
        grid_spec=grid_spec,
        out_shape=[jax.ShapeDtypeStruct((tp, d), F32), jax.ShapeDtypeStruct((ts, d), F32)],
        compiler_params=pltpu.CompilerParams(
            dimension_semantics=("arbitrary",),
            vmem_limit_bytes=VMEM_LIMIT_BYTES),
    )(dest, y_sorted, h, route)


def _routing_tables(route, *, tm, n_tiles):
    t_all = route.shape[0]
    e_flat = jnp.concatenate([route[:, 0], route[:, 1]]).astype(jnp.int32)
    onehot = (e_flat[:, None] == jnp.arange(N_EXPERTS, dtype=jnp.int32)[None, :]).astype(jnp.int32)
    csum = jnp.cumsum(onehot, axis=0)
    rank = jnp.sum(csum * onehot, axis=1) - 1
    counts = csum[-1]
    tiles_e = (counts + tm - 1) // tm
    tile_end = jnp.cumsum(tiles_e)
    tile_start = tile_end - tiles_e
    dest = (tile_start * tm)[e_flat] + rank
    token = jnp.arange(2 * t_all, dtype=jnp.int32) % t_all
    src = jnp.zeros((n_tiles * tm,), jnp.int32).at[dest].set(token, unique_indices=True)
    tile_expert = jnp.minimum(
        jnp.sum((jnp.arange(n_tiles, dtype=jnp.int32)[:, None] >= tile_end[None, :]).astype(jnp.int32), axis=1),
        N_EXPERTS - 1).astype(jnp.int32)
    n_used = tile_end[-1:].astype(jnp.int32)
    experts = jnp.arange(N_EXPERTS, dtype=jnp.int32)
    has_rows = counts > 0
    later_with_rows = jnp.logical_and(experts[None, :] > experts[:, None], has_rows[None, :])
    next_e = jnp.min(jnp.where(later_with_rows, experts[None, :], N_EXPERTS), axis=1)
    next_e = jnp.where(next_e == N_EXPERTS, -1, next_e).astype(jnp.int32)
    ordinal = jnp.cumsum(has_rows.astype(jnp.int32)) - 1
    next_expert = next_e[tile_expert]
    weight_slot = (ordinal[tile_expert] % 2).astype(jnp.int32)
    return tile_expert, next_expert, weight_slot, src, n_used, dest.astype(jnp.int32)


def _pad_lanes(v, n=LANES, offset=0):
    out = jnp.zeros((1, n), F32)
    return out.at[0, offset:offset + v.shape[0]].set(v.astype(F32))


def kernel(x_prompt, x_sample, cache_conv_a, state_delta_a, cache_k_sb, cache_v_sb, norm1_g, w_in, conv_a_w, a_log, dt_bias, a_out_norm_g, sb_q_norm_g, sb_k_norm_g, w_branch_a, w_branch_b, w_out, norm2_g, w_group, b_group, w_expert_router, b_expert_router, w1, w3, w2):
    assert norm1_g.shape[0] == 1, "single-layer model"
    bp, lp, d = x_prompt.shape
    bs, ls, _ = x_sample.shape
    past = cache_k_sb.shape[2]
    tp, ts = bp * lp, bs * ls
    nh, hd = N_HEADS, HEAD_DIM
    a_conv = 3 * nh * hd

    wi = w_in[0]
    c_z = a_conv + nh * hd
    c_ba = c_z + 2 * nh
    w_main = jnp.concatenate([wi[:, :c_z], wi[:, c_ba:]], axis=1).astype(BF16)
    w_ba = jnp.pad(wi[:, c_z:c_ba], ((0, 0), (0, LANES - 2 * nh))).astype(BF16)
    alog_pad = _pad_lanes(a_log[0], offset=nh)
    dtb_pad = _pad_lanes(dt_bias[0], offset=nh)
    x_p = x_prompt.reshape(tp, d)
    x_s = x_sample.reshape(ts, d)

    proj_w = (norm1_g, w_main, w_ba, sb_q_norm_g, sb_k_norm_g, alog_pad, dtb_pad)
    conv_in_p, z_p, q_p, k_p, v_p, gates_p, ba_p = _proj(x_p, *proj_w)
    conv_in_s, z_s, q_s, k_s, v_s, gates_s, ba_s = _proj(x_s, *proj_w)

    conv_w = conv_a_w[0]
    hist_p = jnp.zeros((bp, SUBLANES, a_conv), F32)
    s0_p = jnp.zeros((bp, nh, hd, hd), F32)
    oa_p, delta_p = _delta(conv_in_p, 0, hist_p, conv_w, ba_p, z_p, s0_p, a_out_norm_g,
                           batch=bp, length=lp, valid_chunks=DELTA_ROWS // CHUNK)
    pad_rows = DELTA_ROWS - ls

    def pad_stream(a):
        a = a.reshape(bs, ls, a.shape[-1])
        return jnp.pad(a, ((0, 0), (0, pad_rows), (0, 0))).reshape(bs * DELTA_ROWS, a.shape[-1])

    hist_s = jnp.pad(cache_conv_a[0], ((0, 0), (SUBLANES - (CONV_W - 1), 0), (0, 0)))
    oa_s_pad, delta_s = _delta(pad_stream(conv_in_s), 0, hist_s, conv_w, pad_stream(ba_s), pad_stream(z_s),
                               state_delta_a[0], a_out_norm_g,
                               batch=bs, length=DELTA_ROWS, valid_chunks=ls // CHUNK)
    oa_s = oa_s_pad.reshape(bs, DELTA_ROWS, nh * hd)[:, :ls].reshape(ts, nh * hd)

    bk = 256
    ob_p = _sb_attn(q_p, 0, k_p.reshape(bp, lp, nh * hd), v_p.reshape(bp, lp, nh * hd),
                    batch=bp, q_len=lp, tq=2 * bk, bk=bk, q_start=0)
    lk_s = past + ls
    lk_pad = -(-lk_s // bk) * bk
    k_all = jnp.concatenate([cache_k_sb[0].reshape(bs, past, nh * hd), k_s.reshape(bs, ls, nh * hd)], axis=1)
    v_all = jnp.concatenate([cache_v_sb[0].reshape(bs, past, nh * hd), v_s.reshape(bs, ls, nh * hd)], axis=1)
    k_all = jnp.pad(k_all, ((0, 0), (0, lk_pad - lk_s), (0, 0)))
    v_all = jnp.pad(v_all, ((0, 0), (0, lk_pad - lk_s), (0, 0)))
    ob_s = _sb_attn(q_s, 0, k_all, v_all, batch=bs, q_len=ls, tq=ls, bk=bk, q_start=past)

    w_router = jnp.concatenate(
        [w_group[0], jnp.moveaxis(w_expert_router[0], 0, 1).reshape(d, N_EXPERTS)], axis=1)
    w_router = jnp.pad(w_router, ((0, 0), (0, LANES - N_GROUPS - N_EXPERTS)))
    wr_hi = w_router.astype(BF16)
    wr_lo = (w_router - wr_hi.astype(F32)).astype(BF16)
    b_router = _pad_lanes(jnp.concatenate([b_group[0], b_expert_router[0].reshape(N_EXPERTS)]))
    h_all, hn_all, route = _merge(oa_p, oa_s, ob_p, ob_s, gates_p, gates_s, x_p, x_s,
                                  w_branch_a[0].astype(BF16), w_branch_b[0].astype(BF16),
                                  w_out[0].astype(BF16), norm2_g, wr_hi, wr_lo, b_router)

    t_all = tp + ts
    n_tiles = (2 * t_all + N_EXPERTS * (MOE_TM - 1)) // MOE_TM + 1
    tile_expert, next_expert, weight_slot, src, n_used, dest = _routing_tables(route, tm=MOE_TM, n_tiles=n_tiles)
    ew1 = w1[0].reshape(N_EXPERTS, d, D_EXPERT)
    ew3 = w3[0].reshape(N_EXPERTS, d, D_EXPERT)
    ew2 = w2[0].reshape(N_EXPERTS, D_EXPERT, d)
    y_sorted = _moe_ffn(tile_expert, next_expert, weight_slot, src, n_used, hn_all, ew1, ew3, ew2,
                        n_tiles=n_tiles)
    y_p, y_s = _combine(dest, y_sorted, h_all, route, tp=tp, ts=ts)

    def last_rows(conv_in, n, length):
        return jnp.stack([conv_in[(b + 1) * length - (CONV_W - 1):(b + 1) * length]
                          for b in range(n)]).astype(F32)
    conv_p = last_rows(conv_in_p, bp, lp)
    conv_s = last_rows(conv_in_s, bs, ls)
    return (y_p.reshape(bp, lp, d), y_s.reshape(bs, ls, d),
            conv_p[None], delta_p[None],
            k_p.reshape(1, bp, lp, nh, hd), v_p.reshape(1, bp, lp, nh, hd),
            conv_s[None], delta_s[None],
            k_s.reshape(1, bs, ls, nh, hd), v_s.reshape(1, bs, ls, nh, hd))
```
